```python
import math
import jax, jax.numpy as jnp
from jax import lax
import numpy as np

D_MODEL = 1024
BATCH = 16
SEQ = 256
DEPTH = 2
DEC_BATCH = 8
DEC_SEQ = 1024
PAST_LEN = 256

GRID_W = 64
N_EVEN = (DEPTH + 1) // 2
N_ODD = DEPTH // 2
A_DIM = D_MODEL // 2
A_GROUPS = 4
A_GROUP_DIM = A_DIM // A_GROUPS
CHUNK = 128
B_DIM = D_MODEL // 2
B_CONV = 31
C_DIM = D_MODEL // 2
C_HEAD_DIM = 64
C_HEADS = C_DIM // C_HEAD_DIM
DECAY_LORA = 64
ICLR_LORA = 64
GATE_LORA = 128
GN_EPS = 64e-5
HEAD_DIM = 64
D_HEADS = (D_MODEL // 2) // HEAD_DIM
D_KV_HEADS = 2
D_Q_DIM = D_HEADS * HEAD_DIM
D_KV_DIM = D_KV_HEADS * HEAD_DIM
ROPE_AXIS_DIM = HEAD_DIM // 2
ROPE_THETA = 10000.0
Q_BLOCK = 128
D_FF = 2816
FFN_CONV = 3
EPS = 1e-6

kernel_name = 'hybrid_dit_gmlp_conformer_rwkv7_gqa_step'


def rmsnorm(x, g, eps=EPS):
    xf = x.astype(jnp.float32)
    y = xf * lax.rsqrt(jnp.mean(xf * xf, -1, keepdims=True) + eps)
    return (y * g.astype(jnp.float32)).astype(x.dtype)


def layernorm(x, g, b, eps=EPS):
    xf = x.astype(jnp.float32)
    xc = xf - jnp.mean(xf, -1, keepdims=True)
    y = xc * lax.rsqrt(jnp.mean(xc * xc, -1, keepdims=True) + eps)
    return (y * g.astype(jnp.float32) + b.astype(jnp.float32)).astype(x.dtype)


def adaln(x, g, shift, scale):
    return rmsnorm(x, g) * (1 + scale) + shift


def modulation(cvec, w_mod, b_mod):
    m = jax.nn.silu(cvec) @ w_mod + b_mod
    return jnp.split(m[:, None, :], 6, axis=-1)


def dwconv(x, w, b):
    k = w.shape[0]
    y = lax.conv_general_dilated(x, w[:, None, :].astype(x.dtype), window_strides=(1,),
                                 padding=[(k // 2, k // 2)],
                                 dimension_numbers=('NWC', 'WIO', 'NWC'),
                                 feature_group_count=x.shape[-1])
    return y + b.astype(x.dtype)


def token_shift_delta(x):
    xp = jnp.pad(x, ((0, 0), (1, 1), (0, 0)))
    return 0.5 * (xp[:, :-2] + xp[:, 2:]) - x


def grid_positions(n_tokens):
    rows = n_tokens // GRID_W
    row = jnp.repeat(jnp.arange(rows), GRID_W)
    col = jnp.tile(jnp.arange(GRID_W), rows)
    return row, col


def rope_axis(x, pos):
    half = x.shape[-1] // 2
    inv = ROPE_THETA ** (-jnp.arange(half, dtype=jnp.float32) / half)
    ang = pos.astype(jnp.float32)[:, None] * inv[None, :]
    cos = jnp.cos(ang)[None, :, None, :]
    sin = jnp.sin(ang)[None, :, None, :]
    x1, x2 = x[..., :half], x[..., half:]
    return jnp.concatenate([x1 * cos - x2 * sin, x1 * sin + x2 * cos], -1)


def rope2d(x):
    row, col = grid_positions(x.shape[1])
    xf = x.astype(jnp.float32)
    y = jnp.concatenate([rope_axis(xf[..., :ROPE_AXIS_DIM], row),
                         rope_axis(xf[..., ROPE_AXIS_DIM:], col)], -1)
    return y.astype(x.dtype)


def mixer_ab(h, w_in, w_out, v_g, v_b, ws, bs, conv_w, conv_b, norm_g):
    bsz, n, _ = h.shape
    z = h @ w_in
    za = jax.nn.gelu(z[..., :2 * A_DIM])
    u, v = za[..., :A_DIM], za[..., A_DIM:]
    v = layernorm(v, v_g, v_b).reshape(bsz, n // CHUNK, CHUNK, A_GROUPS, A_GROUP_DIM)
    v = jnp.einsum('gpq,bnqgc->bnpgc', ws, v) + bs.T[None, None, :, :, None]
    ya = u * v.reshape(bsz, n, A_DIM)
    zb = z[..., 2 * A_DIM:]
    glu = zb[..., :B_DIM] * jax.nn.sigmoid(zb[..., B_DIM:])
    yb = jax.nn.silu(rmsnorm(dwconv(glu, conv_w, conv_b), norm_g))
    return jnp.concatenate([ya, yb], -1) @ w_out


def wkv_scan(s0, r, w, k, v, kk, a, reverse):
    tm = lambda t: jnp.swapaxes(t, 0, 1)

    def step(s, inp):
        r_t, w_t, k_t, v_t, kk_t, a_t = inp
        sa = jnp.einsum('bhij,bhj->bhi', s, -kk_t)
        s = s * w_t[:, :, None, :] + sa[..., None] * (kk_t * a_t)[:, :, None, :] + v_t[..., None] * k_t[:, :, None, :]
        return s, jnp.einsum('bhij,bhj->bhi', s, r_t)

    s_fin, ys = lax.scan(step, s0, (tm(r), tm(w), tm(k), tm(v), tm(kk), tm(a)), reverse=reverse)
    return s_fin, tm(ys)


def rwkv_time_mix(h, r_p, k_p, v_p, s0_f, s0_b, mu_rkv, mu_wag, w0, w1, w2, a0, a1, a2,
                  g1, g2, k_k, k_a, r_k, ln_g, ln_b):
    bsz, n, _ = h.shape
    f32 = jnp.float32
    heads = lambda t: t.astype(f32).reshape(bsz, n, C_HEADS, C_HEAD_DIM)
    r = heads(r_p + token_shift_delta(r_p) * mu_rkv[0])
    k = heads(k_p + token_shift_delta(k_p) * mu_rkv[1])
    v = heads(v_p + token_shift_delta(v_p) * mu_rkv[2])
    dh = token_shift_delta(h)
    xw = h + dh * mu_wag[0]
    xa = h + dh * mu_wag[1]
    xg = h + dh * mu_wag[2]
    g = jax.nn.sigmoid(xg @ g1) @ g2
    kk = k * k_k.astype(f32).reshape(C_HEADS, C_HEAD_DIM)
    kk = kk * lax.rsqrt(jnp.sum(kk * kk, -1, keepdims=True) + 1e-12)
    k_a_h = k_a.astype(f32).reshape(C_HEADS, C_HEAD_DIM)

    def direction(d, s0, reverse):
        logw = -jax.nn.softplus(-(w0[d] + jnp.tanh(xw @ w1[d]) @ w2[d])) - 0.5
        decay = heads(jnp.exp(-jnp.exp(logw.astype(f32))))
        a = heads(jax.nn.sigmoid(a0[d] + (xa @ a1[d]) @ a2[d]))
        kd = k * (1 + (a - 1) * k_a_h)
        s_fin, y = wkv_scan(s0.astype(f32), r, decay, kd, v, kk, a, reverse)
        return y, kd, s_fin

    y_f, kd_f, s_f = direction(0, s0_f, False)
    y_b, kd_b, s_b = direction(1, s0_b, True)
    y = y_f + y_b
    yc = y - jnp.mean(y, -1, keepdims=True)
    yn = yc * lax.rsqrt(jnp.mean(yc * yc, -1, keepdims=True) + GN_EPS)
    yn = yn.reshape(bsz, n, C_DIM) * ln_g.astype(f32) + ln_b.astype(f32)
    bonus = jnp.sum(r * (kd_f + kd_b) * r_k.astype(f32), -1, keepdims=True) * v
    out = (yn + bonus.reshape(bsz, n, C_DIM)) * g.astype(f32)
    return out.astype(h.dtype), s_f, s_b


def gqa_attention(q, k, v):
    bsz, lq, nh, hd = q.shape
    grp = nh // D_KV_HEADS
    qb = (q * (hd ** -0.5)).reshape(bsz, lq // Q_BLOCK, Q_BLOCK, D_KV_HEADS, grp, hd)
    qb = jnp.moveaxis(qb, 1, 0)

    def block(qblk):
        s = jnp.einsum('bqkgd,blkd->bkgql', qblk, k, preferred_element_type=jnp.float32)
        p = jax.nn.softmax(s, axis=-1).astype(v.dtype)
        return jnp.einsum('bkgql,blkd->bqkgd', p, v)

    o = lax.map(block, qb)
    return jnp.moveaxis(o, 0, 1).reshape(bsz, lq, nh * hd)


def mixer_cd(h, w_in, w_out, s0_f, s0_b, ctx_k, ctx_v, cp, q_g, k_g, latent):
    bsz, n, _ = h.shape
    z = h @ w_in
    r_p, k_p, v_p, q, kd, vd = jnp.split(
        z, [C_DIM, 2 * C_DIM, 3 * C_DIM, 3 * C_DIM + D_Q_DIM, 3 * C_DIM + D_Q_DIM + D_KV_DIM], axis=-1)
    yc, s_f, s_b = rwkv_time_mix(h, r_p, k_p, v_p, s0_f, s0_b, *cp)
    q = rmsnorm(q.reshape(bsz, n, D_HEADS, HEAD_DIM), q_g)
    kd = rmsnorm(kd.reshape(bsz, n, D_KV_HEADS, HEAD_DIM), k_g)
    vd = vd.reshape(bsz, n, D_KV_HEADS, HEAD_DIM)
    if latent:
        q = rope2d(q)
        keys = jnp.concatenate([ctx_k.astype(kd.dtype), rope2d(kd)], axis=1)
        vals = jnp.concatenate([ctx_v.astype(vd.dtype), vd], axis=1)
    else:
        keys, vals = kd, vd
    yd = gqa_attention(q, keys, vals)
    out = jnp.concatenate([yc, yd], -1) @ w_out
    return out, kd, vd, s_f, s_b


def conv_ffn(h, w_up, cw, cb, w_down):
    z = dwconv(h @ w_up, cw, cb)
    return (jax.nn.silu(z[..., D_FF:]) * z[..., :D_FF]) @ w_down


def setup_inputs(seed: int = 0) -> dict:
    key = jax.random.key(seed)
    ks = iter(jax.random.split(key, 64))
    f32 = jnp.float32
    nrm = lambda shape, scale: jax.random.normal(next(ks), shape, f32) * scale
    lin = lambda shape: nrm(shape, shape[-2] ** -0.5)
    gain = lambda shape: 1.0 + nrm(shape, 0.02)
    uni = lambda shape: jax.random.uniform(next(ks), shape, f32)
    return {
        'x_prompt': nrm((BATCH, SEQ, D_MODEL), 1.0),
        'x_sample': nrm((DEC_BATCH, DEC_SEQ, D_MODEL), 1.0),
        'cache_k': nrm((DEC_BATCH, N_ODD, PAST_LEN, D_KV_HEADS, HEAD_DIM), 1.0),
        'cache_v': nrm((DEC_BATCH, N_ODD, PAST_LEN, D_KV_HEADS, HEAD_DIM), 1.0),
        'state_wkv_fwd': nrm((DEC_BATCH, N_ODD, C_HEADS, C_HEAD_DIM, C_HEAD_DIM), 0.5),
        'state_wkv_bwd': nrm((DEC_BATCH, N_ODD, C_HEADS, C_HEAD_DIM, C_HEAD_DIM), 0.5),
        'c': nrm((DEC_BATCH, D_MODEL), 1.0),
        'c_ctx': nrm((D_MODEL,), 1.0),
        'mod_w': lin((DEPTH, D_MODEL, 6 * D_MODEL)),
        'mod_b': nrm((DEPTH, 6 * D_MODEL), 0.02),
        'norm1_g': gain((DEPTH, D_MODEL)),
        'norm2_g': gain((DEPTH, D_MODEL)),
        'w_out': lin((DEPTH, D_MODEL, D_MODEL)),
        'ffn_up': lin((DEPTH, D_MODEL, 2 * D_FF)),
        'ffn_conv_w': nrm((DEPTH, FFN_CONV, 2 * D_FF), FFN_CONV ** -0.5),
        'ffn_conv_b': nrm((DEPTH, 2 * D_FF), 0.02),
        'ffn_down': lin((DEPTH, D_FF, D_MODEL)),
        'final_g': gain((D_MODEL,)),
        'ab_w_in': lin((N_EVEN, D_MODEL, 2 * A_DIM + 2 * B_DIM)),
        'a_vnorm_g': gain((N_EVEN, A_DIM)),
        'a_vnorm_b': nrm((N_EVEN, A_DIM), 0.02),
        'a_ws': nrm((N_EVEN, A_GROUPS, CHUNK, CHUNK), CHUNK ** -0.5),
        'a_bs': gain((N_EVEN, A_GROUPS, CHUNK)),
        'b_conv_w': nrm((N_EVEN, B_CONV, B_DIM), B_CONV ** -0.5),
        'b_conv_b': nrm((N_EVEN, B_DIM), 0.02),
        'b_norm_g': gain((N_EVEN, B_DIM)),
        'cd_w_in': lin((N_ODD, D_MODEL, 3 * C_DIM + D_Q_DIM + 2 * D_KV_DIM)),
        'c_mu_rkv': uni((N_ODD, 3, C_DIM)),
        'c_mu_wag': uni((N_ODD, 3, D_MODEL)),
        'c_w0': nrm((N_ODD, 2, C_DIM), 0.5),
        'c_w1': lin((N_ODD, 2, D_MODEL, DECAY_LORA)),
        'c_w2': lin((N_ODD, 2, DECAY_LORA, C_DIM)),
        'c_a0': nrm((N_ODD, 2, C_DIM), 0.1),
        'c_a1': lin((N_ODD, 2, D_MODEL, ICLR_LORA)),
        'c_a2': lin((N_ODD, 2, ICLR_LORA, C_DIM)),
        'c_g1': lin((N_ODD, D_MODEL, GATE_LORA)),
        'c_g2': lin((N_ODD, GATE_LORA, C_DIM)),
        'c_k_k': 0.85 + nrm((N_ODD, C_DIM), 0.02),
        'c_k_a': gain((N_ODD, C_DIM)),
        'c_r_k': nrm((N_ODD, C_HEADS, C_HEAD_DIM), 0.1),
        'c_ln_g': gain((N_ODD, C_DIM)),
        'c_ln_b': nrm((N_ODD, C_DIM), 0.02),
        'd_q_g': gain((N_ODD, HEAD_DIM)),
        'd_k_g': gain((N_ODD, HEAD_DIM)),
    }


def reference(x_prompt, x_sample, cache_k, cache_v, state_wkv_fwd, state_wkv_bwd, c, c_ctx,
              mod_w, mod_b, norm1_g, norm2_g, w_out, ffn_up, ffn_conv_w, ffn_conv_b, ffn_down, final_g,
              ab_w_in, a_vnorm_g, a_vnorm_b, a_ws, a_bs, b_conv_w, b_conv_b, b_norm_g,
              cd_w_in, c_mu_rkv, c_mu_wag, c_w0, c_w1, c_w2, c_a0, c_a1, c_a2, c_g1, c_g2,
              c_k_k, c_k_a, c_r_k, c_ln_g, c_ln_b, d_q_g, d_k_g):
    xc, xl = x_prompt, x_sample
    z0 = jnp.zeros((x_prompt.shape[0], C_HEADS, C_HEAD_DIM, C_HEAD_DIM), jnp.float32)
    new_k, new_v, new_sf, new_sb = [], [], [], []
    for layer in range(DEPTH):
        mc = modulation(c_ctx[None, :], mod_w[layer], mod_b[layer])
        ml = modulation(c, mod_w[layer], mod_b[layer])
        hc = adaln(xc, norm1_g[layer], mc[0], mc[1])
        hl = adaln(xl, norm1_g[layer], ml[0], ml[1])
        i = layer // 2
        if layer % 2 == 0:
            ab = (ab_w_in[i], w_out[layer], a_vnorm_g[i], a_vnorm_b[i], a_ws[i], a_bs[i],
                  b_conv_w[i], b_conv_b[i], b_norm_g[i])
            oc = mixer_ab(hc, *ab)
            ol = mixer_ab(hl, *ab)
        else:
            cp = (c_mu_rkv[i], c_mu_wag[i], c_w0[i], c_w1[i], c_w2[i], c_a0[i], c_a1[i], c_a2[i],
                  c_g1[i], c_g2[i], c_k_k[i], c_k_a[i], c_r_k[i], c_ln_g[i], c_ln_b[i])
            oc, kc, vc, sfc, sbc = mixer_cd(hc, cd_w_in[i], w_out[layer], z0, z0, None, None,
                                            cp, d_q_g[i], d_k_g[i], False)
            ol, _, _, _, _ = mixer_cd(hl, cd_w_in[i], w_out[layer], state_wkv_fwd[:, i], state_wkv_bwd[:, i],
                                      cache_k[:, i], cache_v[:, i], cp, d_q_g[i], d_k_g[i], True)
            new_k.append(kc)
            new_v.append(vc)
            new_sf.append(sfc.astype(x_prompt.dtype))
            new_sb.append(sbc.astype(x_prompt.dtype))
        xc = xc + mc[2] * oc
        xl = xl + ml[2] * ol
        ffn = (ffn_up[layer], ffn_conv_w[layer], ffn_conv_b[layer], ffn_down[layer])
        xc = xc + mc[5] * conv_ffn(adaln(xc, norm2_g[layer], mc[3], mc[4]), *ffn)
        xl = xl + ml[5] * conv_ffn(adaln(xl, norm2_g[layer], ml[3], ml[4]), *ffn)
    y_prompt = rmsnorm(xc, final_g)
    y_sample = rmsnorm(xl, final_g)
    return (y_prompt, y_sample, jnp.stack(new_k, axis=1), jnp.stack(new_v, axis=1),
            jnp.stack(new_sf, axis=1), jnp.stack(new_sb, axis=1))
```

```python
import functools
import math

import jax
import jax.numpy as jnp
import numpy as np
from jax import lax
from jax.experimental import pallas as pl
from jax.experimental.pallas import tpu as pltpu

F32 = jnp.float32
BF16 = jnp.bfloat16

D = 1024
N_CTX, L_CTX = 16, 256
N_LAT, L_LAT = 8, 1024
ROWS_CTX = N_CTX * L_CTX
ROWS_LAT = N_LAT * L_LAT
ROWS = ROWS_CTX + ROWS_LAT
PAST = 256
GRID_W = 64
EPS = 1e-6
GN_EPS = 64e-5
HALF = 512
HD = 64
NPAIR = HALF // 128
D_FF = 2816
B_CONV = 31
CHUNK = 128
A_GROUPS = 4

TM = 1024
NT = ROWS // TM
NT_CTX = ROWS_CTX // TM
SEG = 256
NSEG = TM // SEG
HALO = 16
TS = 512
NS = ROWS // TS
NS_CTX = ROWS_CTX // TS
FB = 256
NJ = D_FF // FB
SC = 64
QB = 256
VMEM_LIMIT = 56 * 1024 * 1024

_NT_DIMS = (((1,), (1,)), ((), ()))
_TN_DIMS = (((0,), (0,)), ((), ()))


def _dot(a, b):
    return jnp.dot(a, b, preferred_element_type=F32)


def _dot_nt(a, b):
    return lax.dot_general(a, b, _NT_DIMS, preferred_element_type=F32)


def _dot_tn(a, b):
    return lax.dot_general(a, b, _TN_DIMS, preferred_element_type=F32)


def _rms(x, eps=EPS):
    return x * lax.rsqrt(jnp.mean(x * x, -1, keepdims=True) + eps)


def _sigmoid(x):
    return 1.0 / (1.0 + jnp.exp(-x))


def _silu(x):
    return x * _sigmoid(x)


def _gelu_tanh(x):
    return 0.5 * x * (1.0 + jnp.tanh(math.sqrt(2.0 / math.pi) * (x + 0.044715 * (x * x * x))))


def _const_spec(shape):
    return pl.BlockSpec(shape, lambda *_: (0,) * len(shape))


def _params(sem):
    return pltpu.CompilerParams(dimension_semantics=sem, vmem_limit_bytes=VMEM_LIMIT)


def _mod_body(c_ref, w_ref, b_ref, o_ref):
    s = _silu(c_ref[...])
    o_ref[0] = jnp.dot(s, w_ref[0], preferred_element_type=F32,
                       precision=lax.Precision.HIGHEST) + b_ref[0]


def _modulation(cvec, mod_w, mod_b):
    depth = mod_w.shape[0]
    nb = 1536
    return pl.pallas_call(
        _mod_body,
        out_shape=jax.ShapeDtypeStruct((depth, 16, 6 * D), F32),
        grid=(depth, 6 * D // nb),
        in_specs=[pl.BlockSpec((16, D), lambda l, j: (0, 0)),
                  pl.BlockSpec((1, D, nb), lambda l, j: (l, 0, j)),
                  pl.BlockSpec((1, 1, nb), lambda l, j: (l, 0, j))],
        out_specs=pl.BlockSpec((1, 16, nb), lambda l, j: (l, 0, j)),
        compiler_params=_params(("arbitrary", "arbitrary")),
        name="modulation",
    )(cvec, mod_w, mod_b.reshape(depth, 1, 6 * D))


def _mod_index(tiles_ctx):
    def idx(i, tile_rows):
        return jnp.where(i >= tiles_ctx, (i - tiles_ctx) // (L_LAT // tile_rows) + 1, 0)
    return idx


def _mix_ab_body(x_ref, mod_ref, g_ref, win_ref, wout_ref, ws_ref, bs_ref, vg_ref, vb_ref,
                 cw_ref, cb_ref, ng_ref, o_ref, cat_ref, pad_ref):
    i = pl.program_id(0)
    lat = (i >= NT_CTX).astype(F32)
    shift = mod_ref[0, 0:1, :]
    scale = mod_ref[0, 1:2, :]
    gate = mod_ref[0, 2:3, :]

    def phase1(s, carry):
        r0 = pl.multiple_of(s * SEG, SEG)
        x = x_ref[pl.ds(r0, SEG), :]
        h = (_rms(x) * g_ref[...] * (1.0 + scale) + shift).astype(BF16)
        z = _dot(h, win_ref[...])
        za = _gelu_tanh(z[:, :2 * HALF])
        u = za[:, :HALF]
        v = za[:, HALF:]
        vc = v - jnp.mean(v, -1, keepdims=True)
        vn = vc * lax.rsqrt(jnp.mean(vc * vc, -1, keepdims=True) + EPS) * vg_ref[...] + vb_ref[...]
        vnb = vn.astype(BF16)
        for n in range(SEG // CHUNK):
            parts = []
            for g in range(A_GROUPS):
                vv = vnb[n * CHUNK:(n + 1) * CHUNK, g * 128:(g + 1) * 128]
                parts.append(_dot(ws_ref[g], vv) + bs_ref[g])
            vp = jnp.concatenate(parts, axis=1)
            ya = u[n * CHUNK:(n + 1) * CHUNK, :] * vp
            cat_ref[pl.ds(pl.multiple_of(r0 + n * CHUNK, CHUNK), CHUNK), 0:HALF] = ya.astype(BF16)
        zb = z[:, 2 * HALF:]
        pad_ref[s, HALO:HALO + SEG, :] = zb[:, :HALF] * _sigmoid(zb[:, HALF:])
        return carry

    lax.fori_loop(0, NSEG, phase1, 0)

    zeros_h = jnp.zeros((HALO, HALF), F32)
    for s in range(NSEG):
        left = pad_ref[s - 1, SEG:SEG + HALO, :] * lat if s > 0 else zeros_h
        right = pad_ref[s + 1, HALO:2 * HALO, :] * lat if s < NSEG - 1 else zeros_h
        pad_ref[s, 0:HALO, :] = left
        pad_ref[s, HALO + SEG:2 * HALO + SEG, :] = right

    RB = 32

    def phase3(s, carry):
        r0 = pl.multiple_of(s * SEG, SEG)
        for rb in range(SEG // RB):
            acc = jnp.zeros((RB, HALF), F32) + cb_ref[...]
            for k in range(B_CONV):
                off = HALO - B_CONV // 2 + k + rb * RB
                acc = acc + cw_ref[k:k + 1, :] * pad_ref[s, off:off + RB, :]
            yb = _silu(_rms(acc) * ng_ref[...])
            cat_ref[pl.ds(pl.multiple_of(r0 + rb * RB, RB), RB), HALF:2 * HALF] = yb.astype(BF16)
        out = _dot(cat_ref[pl.ds(r0, SEG), :], wout_ref[...])
        o_ref[pl.ds(r0, SEG), :] = x_ref[pl.ds(r0, SEG), :] + gate * out
        return carry

    lax.fori_loop(0, NSEG, phase3, 0)


def _mix_ab(x, mod, g, win, wout, ws, bs, vg, vb, cw, cb, ng):
    mi = _mod_index(NT_CTX)
    return pl.pallas_call(
        _mix_ab_body,
        out_shape=jax.ShapeDtypeStruct((ROWS, D), F32),
        grid=(NT,),
        in_specs=[pl.BlockSpec((TM, D), lambda i: (i, 0)),
                  pl.BlockSpec((1, 6, D), lambda i: (mi(i, TM), 0, 0)),
                  _const_spec((1, D)), _const_spec((D, 4 * HALF)), _const_spec((D, D)),
                  _const_spec((A_GROUPS, CHUNK, CHUNK)), _const_spec((A_GROUPS, CHUNK, 128)),
                  _const_spec((1, HALF)), _const_spec((1, HALF)),
                  _const_spec((B_CONV + 1, HALF)), _const_spec((1, HALF)), _const_spec((1, HALF))],
        out_specs=pl.BlockSpec((TM, D), lambda i: (i, 0)),
        scratch_shapes=[pltpu.VMEM((TM, D), BF16),
                        pltpu.VMEM((NSEG, SEG + 2 * HALO, HALF), F32)],
        compiler_params=_params(("arbitrary",)),
        name="mix_ab",
    )(x, mod, g, win, wout, ws, bs, vg, vb, cw, cb, ng)


def _ffn_body(x_ref, mod_ref, g_ref, upa_ref, upb_ref, cwa_ref, cwb_ref, cba_ref, cbb_ref,
              down_ref, fg_ref, o_ref, h_ref, acc_ref, *, final):
    i = pl.program_id(0)
    j = pl.program_id(1)

    @pl.when(j == 0)
    def _():
        shift = mod_ref[0, 3:4, :]
        scale = mod_ref[0, 4:5, :]
        h_ref[...] = (_rms(x_ref[...]) * g_ref[...] * (1.0 + scale) + shift).astype(BF16)
        acc_ref[...] = jnp.zeros_like(acc_ref)

    seq = jnp.where(i >= NT_CTX, L_LAT, L_CTX)
    rin = lax.broadcasted_iota(jnp.int32, (TM, FB), 0) & (seq - 1)
    has_prev = rin != 0
    has_next = rin != seq - 1
    h = h_ref[...]

    def branch(up_ref, cw_ref, cb_ref):
        z = _dot(h, up_ref[...])
        zp = jnp.where(has_prev, pltpu.roll(z, 1, 0), 0.0)
        zn = jnp.where(has_next, pltpu.roll(z, TM - 1, 0), 0.0)
        return cw_ref[0:1, :] * zp + cw_ref[1:2, :] * z + cw_ref[2:3, :] * zn + cb_ref[...]

    za = branch(upa_ref, cwa_ref, cba_ref)
    zb = branch(upb_ref, cwb_ref, cbb_ref)
    acc_ref[...] += _dot((_silu(zb) * za).astype(BF16), down_ref[...])

    @pl.when(j == NJ - 1)
    def _():
        y = x_ref[...] + mod_ref[0, 5:6, :] * acc_ref[...]
        if final:
            y = _rms(y) * fg_ref[...]
        o_ref[...] = y


def _ffn(x, mod, g, up, cw, cb, down, fg, final):
    mi = _mod_index(NT_CTX)
    return pl.pallas_call(
        functools.partial(_ffn_body, final=final),
        out_shape=jax.ShapeDtypeStruct((ROWS, D), F32),
        grid=(NT, NJ),
        in_specs=[pl.BlockSpec((TM, D), lambda i, j: (i, 0)),
                  pl.BlockSpec((1, 6, D), lambda i, j: (mi(i, TM), 0, 0)),
                  _const_spec((1, D)),
                  pl.BlockSpec((D, FB), lambda i, j: (0, j)),
                  pl.BlockSpec((D, FB), lambda i, j: (0, j + NJ)),
                  pl.BlockSpec((3, FB), lambda i, j: (0, j)),
                  pl.BlockSpec((3, FB), lambda i, j: (0, j + NJ)),
                  pl.BlockSpec((1, FB), lambda i, j: (0, j)),
                  pl.BlockSpec((1, FB), lambda i, j: (0, j + NJ)),
                  pl.BlockSpec((FB, D), lambda i, j: (j, 0)),
                  _const_spec((1, D))],
        out_specs=pl.BlockSpec((TM, D), lambda i, j: (i, 0)),
        scratch_shapes=[pltpu.VMEM((TM, D), BF16), pltpu.VMEM((TM, D), F32)],
        compiler_params=_params(("arbitrary", "arbitrary")),
        name="conv_ffn",
    )(x, mod, g, up, up, cw, cw, cb, cb, down, fg)


def _head_sum(x, e_ref):
    return _dot(x.astype(BF16), e_ref[...])


def _swap16(x):
    lane = lax.broadcasted_iota(jnp.int32, x.shape, 1)
    n = x.shape[1]
    return jnp.where((lane & 16) == 0, pltpu.roll(x, n - 16, 1), pltpu.roll(x, 16, 1))


def _pre_cd_body(x_ref, xp_ref, xn_ref, mod_ref, g_ref, win_ref, lw1_ref, lw2_ref, la1_ref, la2_ref,
                 lg1_ref, lg2_ref, mu_rkv_ref, mu_wag_ref, w0_ref, a0_ref, kk_ref, qg_ref, kg_ref,
                 e_ref, cos_ref, sin_ref,
                 r_ref, k_ref, v_ref, kkn_ref, lwf_ref, lwb_ref, af_ref, ab_ref, gg_ref,
                 q_ref, kd_ref, vd_ref, ck_ref, cv_ref):
    i = pl.program_id(0)
    is_lat = i >= NS_CTX
    shift = mod_ref[0, 0:1, :]
    scale = mod_ref[0, 1:2, :]
    gn = g_ref[...]

    def adaln(x):
        return _rms(x) * gn * (1.0 + scale) + shift

    h = adaln(x_ref[...])
    hh = adaln(jnp.concatenate([xp_ref[...], xn_ref[...]], axis=0))
    hb = h.astype(BF16)

    seq = jnp.where(is_lat, L_LAT, L_CTX)

    def delta(z, zh):
        n = z.shape[1]
        r2 = lax.broadcasted_iota(jnp.int32, (TS, n), 0)
        rin2 = (r2 + i * TS) & (seq - 1)
        zp = jnp.where(r2 == 0, zh[7:8, :], pltpu.roll(z, 1, 0))
        zn = jnp.where(r2 == TS - 1, zh[8:9, :], pltpu.roll(z, TS - 1, 0))
        zp = jnp.where(rin2 != 0, zp, 0.0)
        zn = jnp.where(rin2 != seq - 1, zn, 0.0)
        return 0.5 * (zp + zn) - z

    z = _dot(hb, win_ref[:, 0:3 * HALF])
    zh = _dot(hh.astype(BF16), win_ref[:, 0:3 * HALF])
    outs = []
    for c in range(3):
        zc = z[:, c * HALF:(c + 1) * HALF]
        zhc = zh[:, c * HALF:(c + 1) * HALF]
        outs.append(zc + delta(zc, zhc) * mu_rkv_ref[c:c + 1, :])
    r, k, v = outs
    r_ref[...] = r
    k_ref[...] = k
    v_ref[...] = v
    kk = k * kk_ref[...]
    kkn_ref[...] = kk * lax.rsqrt(_head_sum(kk * kk, e_ref) + 1e-12)

    dh = delta(h, hh)
    xw = (h + dh * mu_wag_ref[0:1, :]).astype(BF16)
    xa = (h + dh * mu_wag_ref[1:2, :]).astype(BF16)
    xg = (h + dh * mu_wag_ref[2:3, :]).astype(BF16)
    tw = jnp.tanh(_dot(xw, lw1_ref[...])).astype(BF16)
    ta = _dot(xa, la1_ref[...]).astype(BF16)
    for d, (lw_out, a_out) in enumerate(((lwf_ref, af_ref), (lwb_ref, ab_ref))):
        wl = w0_ref[d:d + 1, :] + _dot(tw[:, d * 64:(d + 1) * 64], lw2_ref[d])
        lw_out[...] = (-math.exp(-0.5)) * _sigmoid(wl)
        a_out[...] = _sigmoid(a0_ref[d:d + 1, :] + _dot(ta[:, d * 64:(d + 1) * 64], la2_ref[d]))
    gg_ref[...] = _dot(_sigmoid(_dot(xg, lg1_ref[...])).astype(BF16), lg2_ref[...])

    lat_f = is_lat.astype(F32)
    cos = cos_ref[...] * lat_f + (1.0 - lat_f)
    sin = sin_ref[...] * lat_f
    zq = _dot(hb, win_ref[:, 3 * HALF:4 * HALF])
    qn = zq * lax.rsqrt(_head_sum(zq * zq, e_ref) * (1.0 / HD) + EPS) * qg_ref[...]
    cos4 = jnp.concatenate([cos] * NPAIR, axis=1)
    sin4 = jnp.concatenate([sin] * NPAIR, axis=1)
    q_ref[...] = ((qn * cos4 + _swap16(qn) * sin4) * (HD ** -0.5)).astype(BF16)

    zk = _dot(hb, win_ref[:, 4 * HALF:4 * HALF + 256])
    kn = zk * lax.rsqrt(_head_sum(zk * zk, e_ref[0:256, 0:256]) * (1.0 / HD) + EPS) * kg_ref[...]
    lane = lax.broadcasted_iota(jnp.int32, (TS, 128), 1)
    ck_ref[...] = jnp.where(lane < HD, kn[:, 0:128], kn[:, 128:256])
    cos2 = jnp.concatenate([cos, cos], axis=1)
    sin2 = jnp.concatenate([sin, sin], axis=1)
    kd_ref[...] = (kn * cos2 + _swap16(kn) * sin2).astype(BF16)
    zv = _dot(hb, win_ref[:, 4 * HALF + 256:4 * HALF + 512])
    cv_ref[...] = jnp.where(lane < HD, zv[:, 0:128], zv[:, 128:256])
    vd_ref[...] = zv.astype(BF16)


def _pre_cd(x, mod, g, win, lw1, lw2, la1, la2, lg1, lg2, mu_rkv, mu_wag, w0, a0, k_k, qg, kg, e, cos, sin):
    mi = _mod_index(NS_CTX)
    nblk8 = ROWS // 8
    row512 = lambda i: (i, 0)
    f32_out = jax.ShapeDtypeStruct((ROWS, HALF), F32)
    out_shape = [f32_out] * 9 + [jax.ShapeDtypeStruct((ROWS, HALF), BF16),
                                 jax.ShapeDtypeStruct((ROWS, 256), BF16),
                                 jax.ShapeDtypeStruct((ROWS, 256), BF16),
                                 jax.ShapeDtypeStruct((ROWS, 128), F32),
                                 jax.ShapeDtypeStruct((ROWS, 128), F32)]
    out_specs = [pl.BlockSpec((TS, HALF), row512)] * 10 + [pl.BlockSpec((TS, 256), row512)] * 2 + \
                [pl.BlockSpec((TS, 128), row512)] * 2
    return pl.pallas_call(
        _pre_cd_body,
        out_shape=out_shape,
        grid=(NS,),
        in_specs=[pl.BlockSpec((TS, D), row512),
                  pl.BlockSpec((8, D), lambda i: (jnp.maximum(i * (TS // 8) - 1, 0), 0)),
                  pl.BlockSpec((8, D), lambda i: (jnp.minimum((i + 1) * (TS // 8), nblk8 - 1), 0)),
                  pl.BlockSpec((1, 6, D), lambda i: (mi(i, TS), 0, 0)),
                  _const_spec((1, D)), _const_spec(win.shape),
                  _const_spec(lw1.shape), _const_spec(lw2.shape), _const_spec(la1.shape), _const_spec(la2.shape),
                  _const_spec(lg1.shape), _const_spec(lg2.shape),
                  _const_spec((3, HALF)), _const_spec((3, D)), _const_spec((2, HALF)), _const_spec((2, HALF)),
                  _const_spec((1, HALF)), _const_spec((1, HALF)), _const_spec((1, 256)),
                  _const_spec((HALF, HALF)),
                  pl.BlockSpec((TS, 128), lambda i: (i % (L_LAT // TS), 0)),
                  pl.BlockSpec((TS, 128), lambda i: (i % (L_LAT // TS), 0))],
        out_specs=out_specs,
        compiler_params=_params(("arbitrary",)),
        name="pre_cd",
    )(x, x, x, mod, g, win, lw1, lw2, la1, la2, lg1, lg2, mu_rkv, mu_wag, w0, a0, k_k, qg, kg, e, cos, sin)


def _scan_unit(rc, kc, vc, kkc, lwc, ac, ka, s_prev, reverse, masks):
    cum_mask, tri_s, tri_i, eye, lane_lo, diag8, offs = masks
    cum = jnp.dot(cum_mask, lwc, preferred_element_type=F32, precision=lax.Precision.HIGHEST)
    cume = cum - lwc
    if reverse:
        tot, mid = cum[0:1, :], cum[SC // 2:SC // 2 + 1, :]
    else:
        tot, mid = cum[SC - 1:SC, :], cum[SC // 2 - 1:SC // 2, :]
    at = -kkc * jnp.exp(cume - mid)
    rt = rc * jnp.exp(cum - mid)
    e3 = jnp.exp(mid - cum)
    kt = kc * (1.0 + (ac - 1.0) * ka) * e3
    bt = kkc * ac * e3

    def stack(x):
        return jnp.concatenate([jnp.where(lane_lo, x, 0.0), jnp.where(lane_lo, 0.0, x)], axis=0)

    a2 = stack(at)
    r2 = stack(rt)
    v2 = stack(vc).astype(BF16)
    kb = jnp.concatenate([stack(kt), stack(bt)], axis=0).astype(BF16)
    lhs = jnp.concatenate([a2, r2], axis=0).astype(BF16)
    rhs = jnp.concatenate([bt, bt, kt, kt], axis=0).astype(BF16)
    p = _dot_nt(lhs, rhs)
    a_ab = jnp.where(tri_s, p[0:128, 0:128], 0.0)
    a_ak = jnp.where(tri_s, p[0:128, 128:256], 0.0)
    a_rb = jnp.where(tri_i, p[128:256, 0:128], 0.0)
    a_rk = jnp.where(tri_i, p[128:256, 128:256], 0.0)

    lk = jnp.where(diag8, a_ab, 0.0)
    tinv = eye + lk
    for _ in range(2):
        lkb = lk.astype(BF16)
        lk = _dot(lkb, lkb)
        tinv = _dot(tinv.astype(BF16), (eye + lk).astype(BF16))
    for off in offs:
        tb = tinv.astype(BF16)
        tinv = tinv + _dot(_dot(tb, jnp.where(off, a_ab, 0.0).astype(BF16)).astype(BF16), tb)

    akv = _dot(a_ak.astype(BF16), v2)
    wu = _dot(tinv.astype(BF16), jnp.concatenate([a2, akv], axis=1).astype(BF16))
    q = jnp.concatenate([jnp.concatenate([jnp.zeros((128, 128), BF16), v2], axis=1),
                         wu.astype(BF16)], axis=0)
    ry = _dot(jnp.concatenate([a_rk, a_rb], axis=1).astype(BF16), q)
    rw2 = r2 + ry[:, 0:128]
    y02 = ry[:, 128:256]
    gh = _dot_tn(q, kb)

    sd = s_prev * jnp.exp(mid)
    sdb = sd.astype(BF16)
    ys = _dot_nt(rw2.astype(BF16), sdb)
    sg = _dot(sdb, gh[0:128, :].astype(BF16))
    s_new = s_prev * jnp.exp(tot) + (sg + gh[128:256, :]) * jnp.exp(tot - mid)
    y2 = y02 + ys
    return y2[0:SC, :] + y2[SC:2 * SC, :], s_new


def _scan_body(tab_ref, rf_ref, kf_ref, vf_ref, kkf_ref, lwf_ref, af_ref,
               rb_ref, kb_ref, vb_ref, kkb_ref, lwb_ref, ab_ref, ka_ref, s0f_ref, s0b_ref,
               yf_ref, yb_ref, sf_ref, sb_ref, st_ref):
    t = pl.program_id(0)
    first = tab_ref[2, t]
    last = tab_ref[3, t]
    is_lat = tab_ref[4, t]

    @pl.when(first == 1)
    def _():
        zero = jnp.zeros((HD, HD), F32)
        for d, s0_ref in enumerate((s0f_ref, s0b_ref)):
            for p in range(NPAIR):
                lo = s0_ref[0, 2 * p] * is_lat.astype(F32)
                hi = s0_ref[0, 2 * p + 1] * is_lat.astype(F32)
                st_ref[d * NPAIR + p] = jnp.concatenate(
                    [jnp.concatenate([lo, zero], axis=1), jnp.concatenate([zero, hi], axis=1)], axis=0)

    ri = lax.broadcasted_iota(jnp.int32, (SC, SC), 0)
    ci = lax.broadcasted_iota(jnp.int32, (SC, SC), 1)
    r2 = lax.broadcasted_iota(jnp.int32, (128, 128), 0)
    c2 = lax.broadcasted_iota(jnp.int32, (128, 128), 1)
    same = (r2 & SC) == (c2 & SC)
    eye = (r2 == c2).astype(F32)
    diag8 = (r2 >> 3) == (c2 >> 3)
    lane_lo = lax.broadcasted_iota(jnp.int32, (SC, 128), 1) < HD

    dirs = ((False, (rf_ref, kf_ref, vf_ref, kkf_ref, lwf_ref, af_ref), yf_ref),
            (True, (rb_ref, kb_ref, vb_ref, kkb_ref, lwb_ref, ab_ref), yb_ref))
    for d, (reverse, refs, y_ref) in enumerate(dirs):
        early, late = (r2, c2) if reverse else (c2, r2)
        offs = tuple(((r2 >> sh + 1) == (c2 >> sh + 1)) & ((late & (1 << sh)) != 0) & ((early & (1 << sh)) == 0)
                     for sh in (3, 4, 5))
        if reverse:
            masks = ((ri <= ci).astype(F32), same & (r2 < c2), same & (r2 <= c2), eye, lane_lo, diag8, offs)
        else:
            masks = ((ri >= ci).astype(F32), same & (r2 > c2), same & (r2 >= c2), eye, lane_lo, diag8, offs)
        for p in range(NPAIR):
            ls = slice(p * 128, (p + 1) * 128)
            rc, kc, vc, kkc, lwc, ac = (ref[:, ls] for ref in refs)
            y, s_new = _scan_unit(rc, kc, vc, kkc, lwc, ac, ka_ref[:, ls], st_ref[d * NPAIR + p],
                                  reverse, masks)
            y_ref[:, ls] = y
            st_ref[d * NPAIR + p] = s_new

    @pl.when(last == 1)
    def _():
        for d, s_out in enumerate((sf_ref, sb_ref)):
            for p in range(NPAIR):
                s = st_ref[d * NPAIR + p]
                s_out[0, 2 * p] = s[0:HD, 0:HD]
                s_out[0, 2 * p + 1] = s[HD:2 * HD, HD:2 * HD]


def _scan_tables():
    fwd, bwd, first, last, lat, seq_ctx, seq_lat = [], [], [], [], [], [], []
    for group, (nb, ln, base) in enumerate(((N_CTX, L_CTX, 0), (N_LAT, L_LAT, ROWS_CTX))):
        nc = ln // SC
        for b in range(nb):
            for c in range(nc):
                blk = (base + b * ln) // SC
                fwd.append(blk + c)
                bwd.append(blk + nc - 1 - c)
                first.append(int(c == 0))
                last.append(int(c == nc - 1))
                lat.append(group)
                seq_ctx.append(b if group == 0 else N_CTX)
                seq_lat.append(b if group == 1 else 0)
    return np.asarray([fwd, bwd, first, last, lat, seq_ctx, seq_lat], np.int32)


def _scan(r, k, v, kk, lwf, lwb, af, ab, k_a, s0f, s0b):
    tab = _scan_tables()
    steps = tab.shape[1]
    fspec = pl.BlockSpec((SC, HALF), lambda t, tab: (tab[0, t], 0))
    bspec = pl.BlockSpec((SC, HALF), lambda t, tab: (tab[1, t], 0))
    s_in = pl.BlockSpec((1, 2 * NPAIR, HD, HD), lambda t, tab: (tab[6, t], 0, 0, 0))
    s_out = pl.BlockSpec((1, 2 * NPAIR, HD, HD), lambda t, tab: (tab[5, t], 0, 0, 0))
    y_shape = jax.ShapeDtypeStruct((ROWS, HALF), F32)
    s_shape = jax.ShapeDtypeStruct((N_CTX + 1, 2 * NPAIR, HD, HD), F32)
    grid_spec = pltpu.PrefetchScalarGridSpec(
        num_scalar_prefetch=1,
        grid=(steps,),
        in_specs=[fspec] * 6 + [bspec] * 6 + [pl.BlockSpec((1, HALF), lambda t, tab: (0, 0)), s_in, s_in],
        out_specs=[fspec, bspec, s_out, s_out],
        scratch_shapes=[pltpu.VMEM((2 * NPAIR, 128, 128), F32)],
    )
    return pl.pallas_call(
        _scan_body,
        out_shape=[y_shape, y_shape, s_shape, s_shape],
        grid_spec=grid_spec,
        compiler_params=_params(("arbitrary",)),
        name="wkv_scan",
    )(jnp.asarray(tab), r, k, v, kk, lwf, af, r, k, v, kk, lwb, ab, k_a, s0f, s0b)


def _attn_body(q_ref, k_ref, v_ref, *rest, cached):
    if cached:
        ck_ref, cv_ref, o_ref = rest
    else:
        (o_ref,) = rest
    lane_lo = lax.broadcasted_iota(jnp.int32, (QB, 128), 1) < HD
    for pr in range(NPAIR):
        qp = q_ref[:, pr * 128:(pr + 1) * 128]
        kvh = (2 * pr) // 4
        ks = slice(kvh * 128, (kvh + 1) * 128)
        kd = k_ref[:, ks]
        vd = v_ref[:, ks]
        halves = []
        for half in range(2):
            qh = jnp.where(lane_lo if half == 0 else jnp.logical_not(lane_lo), qp, jnp.zeros_like(qp))
            s_new = _dot_nt(qh, kd)
            if cached:
                s_old = _dot_nt(qh, ck_ref[0, :, ks])
                m = jnp.maximum(jnp.max(s_new, -1, keepdims=True), jnp.max(s_old, -1, keepdims=True))
                p_new = jnp.exp(s_new - m)
                p_old = jnp.exp(s_old - m)
                den = jnp.sum(p_new, -1, keepdims=True) + jnp.sum(p_old, -1, keepdims=True)
                o = _dot(p_new.astype(BF16), vd) + _dot(p_old.astype(BF16), cv_ref[0, :, ks])
            else:
                m = jnp.max(s_new, -1, keepdims=True)
                p_new = jnp.exp(s_new - m)
                den = jnp.sum(p_new, -1, keepdims=True)
                o = _dot(p_new.astype(BF16), vd)
            halves.append(o / den)
        o_ref[:, pr * 128:(pr + 1) * 128] = jnp.where(lane_lo, halves[0], halves[1]).astype(BF16)


def _attention(q, kd, vd, base, nb, ln, cache=None):
    nq = ln // QB
    blk0 = base // QB
    seq0 = base // ln
    in_specs = [pl.BlockSpec((QB, HALF), lambda b, j: (blk0 + b * nq + j, 0)),
                pl.BlockSpec((ln, 256), lambda b, j: (seq0 + b, 0)),
                pl.BlockSpec((ln, 256), lambda b, j: (seq0 + b, 0))]
    args = [q, kd, vd]
    if cache is not None:
        in_specs += [pl.BlockSpec((1, PAST, 256), lambda b, j: (b, 0, 0))] * 2
        args += list(cache)
    return pl.pallas_call(
        functools.partial(_attn_body, cached=cache is not None),
        out_shape=jax.ShapeDtypeStruct((nb * ln, HALF), BF16),
        grid=(nb, nq),
        in_specs=in_specs,
        out_specs=pl.BlockSpec((QB, HALF), lambda b, j: (b * nq + j, 0)),
        compiler_params=_params(("arbitrary", "arbitrary")),
        name="gqa_cached" if cache is not None else "gqa",
    )(*args)


def _post_cd_body(x_ref, mod_ref, yf_ref, yb_ref, r_ref, k_ref, v_ref, af_ref, ab_ref, gg_ref,
                  atc_ref, atl_ref, wout_ref, e_ref, lng_ref, lnb_ref, rk_ref, ka_ref, o_ref):
    i = pl.program_id(0)
    gate = mod_ref[0, 2:3, :]
    y = yf_ref[...] + yb_ref[...]
    yc = y - _head_sum(y, e_ref) * (1.0 / HD)
    yn = yc * lax.rsqrt(_head_sum(yc * yc, e_ref) * (1.0 / HD) + GN_EPS) * lng_ref[...] + lnb_ref[...]
    kd2 = k_ref[...] * (2.0 + (af_ref[...] + ab_ref[...] - 2.0) * ka_ref[...])
    bonus = _head_sum(r_ref[...] * kd2 * rk_ref[...], e_ref) * v_ref[...]
    yc_out = ((yn + bonus) * gg_ref[...]).astype(BF16)
    att = jnp.where(i >= NS_CTX, atl_ref[...], atc_ref[...])
    out = _dot(yc_out, wout_ref[0:HALF, :]) + _dot(att, wout_ref[HALF:2 * HALF, :])
    o_ref[...] = x_ref[...] + gate * out


def _post_cd(x, mod, yf, yb, r, k, v, af, ab, gg, att_c, att_l, wout, e, lng, lnb, r_k, k_a):
    mi = _mod_index(NS_CTX)
    row = lambda i: (i, 0)
    half = pl.BlockSpec((TS, HALF), row)
    return pl.pallas_call(
        _post_cd_body,
        out_shape=jax.ShapeDtypeStruct((ROWS, D), F32),
        grid=(NS,),
        in_specs=[pl.BlockSpec((TS, D), row),
                  pl.BlockSpec((1, 6, D), lambda i: (mi(i, TS), 0, 0))] + [half] * 8 +
                 [pl.BlockSpec((TS, HALF), lambda i: (jnp.minimum(i, NS_CTX - 1), 0)),
                  pl.BlockSpec((TS, HALF), lambda i: (jnp.maximum(i - NS_CTX, 0), 0)),
                  _const_spec((D, D)), _const_spec((HALF, HALF)),
                  _const_spec((1, HALF)), _const_spec((1, HALF)), _const_spec((1, HALF)), _const_spec((1, HALF))],
        out_specs=pl.BlockSpec((TS, D), row),
        compiler_params=_params(("arbitrary",)),
        name="post_cd",
    )(x, mod, yf, yb, r, k, v, af, ab, gg, att_c, att_l, wout, e, lng, lnb, r_k, k_a)


def _rope_tables():
    half = 16
    inv = 10000.0 ** (-jnp.arange(half, dtype=F32) / half)
    t = jnp.arange(L_LAT)
    cos_parts, sin_parts = [], []
    for pos in (t // GRID_W, t % GRID_W):
        ang = pos.astype(F32)[:, None] * inv[None, :]
        cos_parts += [jnp.cos(ang), jnp.cos(ang)]
        sin_parts += [-jnp.sin(ang), jnp.sin(ang)]
    cos = jnp.concatenate(cos_parts * 2, axis=1)
    sin = jnp.concatenate(sin_parts * 2, axis=1)
    return cos, sin


def kernel(x_prompt, x_sample, cache_k, cache_v, state_wkv_fwd, state_wkv_bwd, c, c_ctx, mod_w, mod_b, norm1_g, norm2_g, w_out, ffn_up, ffn_conv_w, ffn_conv_b, ffn_down, final_g, ab_w_in, a_vnorm_g, a_vnorm_b, a_ws, a_bs, b_conv_w, b_conv_b, b_norm_g, cd_w_in, c_mu_rkv, c_mu_wag, c_w0, c_w1, c_w2, c_a0, c_a1, c_a2, c_g1, c_g2, c_k_k, c_k_a, c_r_k, c_ln_g, c_ln_b, d_q_g, d_k_g):
    row = lambda a: a.reshape(1, -1)
    x = jnp.concatenate([x_prompt.reshape(ROWS_CTX, D), x_sample.reshape(ROWS_LAT, D)], axis=0)
    cvec = jnp.concatenate([c_ctx[None, :], c, jnp.zeros((16 - 1 - N_LAT, D), F32)], axis=0)
    mods = _modulation(cvec, mod_w, mod_b).reshape(2, 16, 6, D)

    bs = jnp.broadcast_to(a_bs[0][:, :, None], (A_GROUPS, CHUNK, 128))
    cw = jnp.concatenate([b_conv_w[0], jnp.zeros((1, HALF), F32)], axis=0)
    x = _mix_ab(x, mods[0], row(norm1_g[0]), ab_w_in[0].astype(BF16), w_out[0].astype(BF16),
                a_ws[0].astype(BF16), bs, row(a_vnorm_g[0]), row(a_vnorm_b[0]),
                cw, row(b_conv_b[0]), row(b_norm_g[0]))
    x = _ffn(x, mods[0], row(norm2_g[0]), ffn_up[0].astype(BF16), ffn_conv_w[0], row(ffn_conv_b[0]),
             ffn_down[0].astype(BF16), row(final_g), final=False)

    w = cd_w_in[0]
    wk = w[:, 4 * HALF:4 * HALF + 128]
    wv = w[:, 4 * HALF + 128:4 * HALF + 256]
    win = jnp.concatenate([w[:, :4 * HALF], wk[:, :HD], wk[:, :HD], wk[:, HD:], wk[:, HD:],
                           wv[:, :HD], wv[:, :HD], wv[:, HD:], wv[:, HD:]], axis=1).astype(BF16)
    lw1 = jnp.concatenate([c_w1[0, 0], c_w1[0, 1]], axis=1).astype(BF16)
    la1 = jnp.concatenate([c_a1[0, 0], c_a1[0, 1]], axis=1).astype(BF16)
    ones_blk = jnp.asarray(np.kron(np.eye(HALF // HD), np.ones((HD, HD))), BF16)
    cos, sin = _rope_tables()
    k_a = row(c_k_a[0])
    (r, k, v, kk, lwf, lwb, af, ab, gg, q, kd, vd, ck, cv) = _pre_cd(
        x, mods[1], row(norm1_g[1]), win, lw1, c_w2[0].astype(BF16), la1, c_a2[0].astype(BF16),
        c_g1[0].astype(BF16), c_g2[0].astype(BF16), c_mu_rkv[0], c_mu_wag[0], c_w0[0], c_a0[0],
        row(c_k_k[0]), jnp.tile(d_q_g[0], HALF // HD)[None, :], jnp.tile(d_k_g[0], 4)[None, :],
        ones_blk, cos, sin)

    yf, yb, s_f, s_b = _scan(r, k, v, kk, lwf, lwb, af, ab, k_a, state_wkv_fwd[:, 0], state_wkv_bwd[:, 0])

    def dup_cache(t):
        t = t[:, 0].astype(BF16)
        return jnp.concatenate([t[:, :, 0], t[:, :, 0], t[:, :, 1], t[:, :, 1]], axis=-1)

    att_c = _attention(q, kd, vd, 0, N_CTX, L_CTX)
    att_l = _attention(q, kd, vd, ROWS_CTX, N_LAT, L_LAT, cache=(dup_cache(cache_k), dup_cache(cache_v)))

    x = _post_cd(x, mods[1], yf, yb, r, k, v, af, ab, gg, att_c, att_l, w_out[1].astype(BF16), ones_blk,
                 row(c_ln_g[0]), row(c_ln_b[0]), row(c_r_k[0]), k_a)
    x = _ffn(x, mods[1], row(norm2_g[1]), ffn_up[1].astype(BF16), ffn_conv_w[1], row(ffn_conv_b[1]),
             ffn_down[1].astype(BF16), row(final_g), final=True)

    y_prompt = x[:ROWS_CTX].reshape(N_CTX, L_CTX, D)
    y_sample = x[ROWS_CTX:].reshape(N_LAT, L_LAT, D)
    new_k = ck[:ROWS_CTX].reshape(N_CTX, 1, L_CTX, 2, HD)
    new_v = cv[:ROWS_CTX].reshape(N_CTX, 1, L_CTX, 2, HD)
    return (y_prompt, y_sample, new_k, new_v,
            s_f[:N_CTX].reshape(N_CTX, 1, 2 * NPAIR, HD, HD), s_b[:N_CTX].reshape(N_CTX, 1, 2 * NPAIR, HD, HD))
```

```python
import functools
import math

import jax
import jax.numpy as jnp
import numpy as np
from jax import lax
from jax.experimental import pallas as pl
from jax.experimental.pallas import tpu as pltpu

F32 = jnp.float32
BF16 = jnp.bfloat16

D = 1024
N_CTX, L_CTX = 16, 256
N_LAT, L_LAT = 8, 1024
ROWS_CTX = N_CTX * L_CTX
ROWS_LAT = N_LAT * L_LAT
ROWS = ROWS_CTX + ROWS_LAT
PAST = 256
GRID_W = 64
EPS = 1e-6
GN_EPS = 64e-5
HALF = 512
HD = 64
NPAIR = HALF // 128
D_FF = 2816
B_CONV = 31
CHUNK = 128
A_GROUPS = 4

TM = 1024
NT = ROWS // TM
NT_CTX = ROWS_CTX // TM
SEG = 256
NSEG = TM // SEG
HALO = 16
TS = 512
NS = ROWS // TS
NS_CTX = ROWS_CTX // TS
FB = 256
NJ = D_FF // FB
SC = 64
QB = 256
VMEM_LIMIT = 56 * 1024 * 1024

_NT_DIMS = (((1,), (1,)), ((), ()))
_TN_DIMS = (((0,), (0,)), ((), ()))


def _dot(a, b):
    return jnp.dot(a, b, preferred_element_type=F32)


def _dot_nt(a, b):
    return lax.dot_general(a, b, _NT_DIMS, preferred_element_type=F32)


def _dot_tn(a, b):
    return lax.dot_general(a, b, _TN_DIMS, preferred_element_type=F32)


def _rms(x, eps=EPS):
    return x * lax.rsqrt(jnp.mean(x * x, -1, keepdims=True) + eps)


def _sigmoid(x):
    return 1.0 / (1.0 + jnp.exp(-x))


def _silu(x):
    return x * _sigmoid(x)


def _gelu_tanh(x):
    return 0.5 * x * (1.0 + jnp.tanh(math.sqrt(2.0 / math.pi) * (x + 0.044715 * (x * x * x))))


def _const_spec(shape):
    return pl.BlockSpec(shape, lambda *_: (0,) * len(shape), pipeline_mode=pl.Buffered(1))


def _params(sem):
    return pltpu.CompilerParams(dimension_semantics=sem, vmem_limit_bytes=VMEM_LIMIT)


def _mod_body(c_ref, w_ref, b_ref, o_ref):
    s = _silu(c_ref[...])
    o_ref[0] = jnp.dot(s, w_ref[0], preferred_element_type=F32,
                       precision=lax.Precision.HIGHEST) + b_ref[0]


def _modulation(cvec, mod_w, mod_b):
    depth = mod_w.shape[0]
    nb = 1536
    return pl.pallas_call(
        _mod_body,
        out_shape=jax.ShapeDtypeStruct((depth, 16, 6 * D), F32),
        grid=(depth, 6 * D // nb),
        in_specs=[pl.BlockSpec((16, D), lambda l, j: (0, 0)),
                  pl.BlockSpec((1, D, nb), lambda l, j: (l, 0, j)),
                  pl.BlockSpec((1, 1, nb), lambda l, j: (l, 0, j))],
        out_specs=pl.BlockSpec((1, 16, nb), lambda l, j: (l, 0, j)),
        compiler_params=_params(("arbitrary", "arbitrary")),
        name="modulation",
    )(cvec, mod_w, mod_b.reshape(depth, 1, 6 * D))


def _mod_index(tiles_ctx):
    def idx(i, tile_rows):
        return jnp.where(i >= tiles_ctx, (i - tiles_ctx) // (L_LAT // tile_rows) + 1, 0)
    return idx


def _mix_ab_body(xc_ref, xl_ref, mod_ref, g_ref, win_ref, wout_ref, ws_ref, bs_ref, vg_ref, vb_ref,
                 cw_ref, cb_ref, ng_ref, o_ref, cat_ref, pad_ref):
    i = pl.program_id(0)
    is_lat = i >= NT_CTX
    lat = is_lat.astype(F32)

    def x_rows(r0):
        return jnp.where(is_lat, xl_ref[pl.ds(r0, SEG), :], xc_ref[pl.ds(r0, SEG), :])

    shift = mod_ref[0, 0:1, :]
    scale = mod_ref[0, 1:2, :]
    gate = mod_ref[0, 2:3, :]

    def phase1(s, carry):
        r0 = pl.multiple_of(s * SEG, SEG)
        x = x_rows(r0)
        h = (_rms(x) * g_ref[...] * (1.0 + scale) + shift).astype(BF16)
        z = _dot(h, win_ref[...])
        za = _gelu_tanh(z[:, :2 * HALF])
        u = za[:, :HALF]
        v = za[:, HALF:]
        vc = v - jnp.mean(v, -1, keepdims=True)
        vn = vc * lax.rsqrt(jnp.mean(vc * vc, -1, keepdims=True) + EPS) * vg_ref[...] + vb_ref[...]
        vnb = vn.astype(BF16)
        for n in range(SEG // CHUNK):
            parts = []
            for g in range(A_GROUPS):
                vv = vnb[n * CHUNK:(n + 1) * CHUNK, g * 128:(g + 1) * 128]
                parts.append(_dot(ws_ref[g], vv) + bs_ref[g])
            vp = jnp.concatenate(parts, axis=1)
            ya = u[n * CHUNK:(n + 1) * CHUNK, :] * vp
            cat_ref[pl.ds(pl.multiple_of(r0 + n * CHUNK, CHUNK), CHUNK), 0:HALF] = ya.astype(BF16)
        zb = z[:, 2 * HALF:]
        pad_ref[s, HALO:HALO + SEG, :] = zb[:, :HALF] * _sigmoid(zb[:, HALF:])
        return carry

    lax.fori_loop(0, NSEG, phase1, 0)

    zeros_h = jnp.zeros((HALO, HALF), F32)
    for s in range(NSEG):
        left = pad_ref[s - 1, SEG:SEG + HALO, :] * lat if s > 0 else zeros_h
        right = pad_ref[s + 1, HALO:2 * HALO, :] * lat if s < NSEG - 1 else zeros_h
        pad_ref[s, 0:HALO, :] = left
        pad_ref[s, HALO + SEG:2 * HALO + SEG, :] = right

    RB = 32

    def phase3(s, carry):
        r0 = pl.multiple_of(s * SEG, SEG)
        for rb in range(SEG // RB):
            acc = jnp.zeros((RB, HALF), F32) + cb_ref[...]
            for k in range(B_CONV):
                off = HALO - B_CONV // 2 + k + rb * RB
                acc = acc + cw_ref[k:k + 1, :] * pad_ref[s, off:off + RB, :]
            yb = _silu(_rms(acc) * ng_ref[...])
            cat_ref[pl.ds(pl.multiple_of(r0 + rb * RB, RB), RB), HALF:2 * HALF] = yb.astype(BF16)
        out = _dot(cat_ref[pl.ds(r0, SEG), :], wout_ref[...])
        o_ref[pl.ds(r0, SEG), :] = x_rows(r0) + gate * out
        return carry

    lax.fori_loop(0, NSEG, phase3, 0)


def _mix_ab(xc, xl, mod, g, win, wout, ws, bs, vg, vb, cw, cb, ng):
    mi = _mod_index(NT_CTX)
    return pl.pallas_call(
        _mix_ab_body,
        out_shape=jax.ShapeDtypeStruct((ROWS, D), F32),
        grid=(NT,),
        in_specs=[pl.BlockSpec((TM, D), lambda i: (jnp.minimum(i, NT_CTX - 1), 0)),
                  pl.BlockSpec((TM, D), lambda i: (jnp.maximum(i - NT_CTX, 0), 0)),
                  pl.BlockSpec((1, 6, D), lambda i: (mi(i, TM), 0, 0)),
                  _const_spec((1, D)), _const_spec((D, 4 * HALF)), _const_spec((D, D)),
                  _const_spec((A_GROUPS, CHUNK, CHUNK)), _const_spec((A_GROUPS, CHUNK, 128)),
                  _const_spec((1, HALF)), _const_spec((1, HALF)),
                  _const_spec((B_CONV + 1, HALF)), _const_spec((1, HALF)), _const_spec((1, HALF))],
        out_specs=pl.BlockSpec((TM, D), lambda i: (i, 0)),
        scratch_shapes=[pltpu.VMEM((TM, D), BF16),
                        pltpu.VMEM((NSEG, SEG + 2 * HALO, HALF), F32)],
        compiler_params=_params(("arbitrary",)),
        name="mix_ab",
    )(xc, xl, mod, g, win, wout, ws, bs, vg, vb, cw, cb, ng)


def _ffn_body(x_ref, mod_ref, g_ref, up_ref, cw_ref, cb_ref, down_ref, fg_ref, o_ref,
              h_ref, gated_ref, cza_ref, czb_ref, *, tile0, final):
    is_ctx = tile0 + pl.program_id(0) < NT_CTX
    shift = mod_ref[0, 3:4, :]
    scale = mod_ref[0, 4:5, :]
    h_ref[...] = (_rms(x_ref[...]) * g_ref[...] * (1.0 + scale) + shift).astype(BF16)

    def up_proj(col):
        return _dot(h_ref[...], up_ref[:, col * FB:(col + 1) * FB])

    def conv_branch(col, z, cz_ref):
        cs = slice(col * FB, (col + 1) * FB)
        w0, w1, w2, b = cw_ref[0:1, cs], cw_ref[1:2, cs], cw_ref[2:3, cs], cb_ref[:, cs]
        cz_ref[...] = w0 * pltpu.roll(z, 1, 0) + w1 * z + w2 * pltpu.roll(z, TM - 1, 0) + b
        for t0 in range(0, TM, L_CTX):
            t1 = t0 + L_CTX - 1
            first = w1 * z[t0:t0 + 1, :] + w2 * z[t0 + 1:t0 + 2, :] + b
            last = w0 * z[t1 - 1:t1, :] + w1 * z[t1:t1 + 1, :] + b
            cz_ref[t0:t0 + 1, :] = first if t0 == 0 else jnp.where(is_ctx, first, cz_ref[t0:t0 + 1, :])
            cz_ref[t1:t1 + 1, :] = last if t1 == TM - 1 else jnp.where(is_ctx, last, cz_ref[t1:t1 + 1, :])

    for j in range(NJ):
        conv_branch(j, up_proj(j), cza_ref)
        conv_branch(NJ + j, up_proj(NJ + j), czb_ref)
        gated_ref[:, j * FB:(j + 1) * FB] = (_silu(czb_ref[...]) * cza_ref[...]).astype(BF16)

    y = x_ref[...] + mod_ref[0, 5:6, :] * _dot(gated_ref[...], down_ref[...])
    if final:
        y = _rms(y) * fg_ref[...]
    o_ref[...] = y


def _ffn(x, tile0, ntiles, mod, g, up, cw, cb, down, fg, final):
    mi = _mod_index(NT_CTX)
    return pl.pallas_call(
        functools.partial(_ffn_body, tile0=tile0, final=final),
        out_shape=jax.ShapeDtypeStruct((ntiles * TM, D), F32),
        grid=(ntiles,),
        in_specs=[pl.BlockSpec((TM, D), lambda i: (tile0 + i, 0)),
                  pl.BlockSpec((1, 6, D), lambda i: (mi(tile0 + i, TM), 0, 0)),
                  _const_spec((1, D)),
                  _const_spec((D, 2 * D_FF)), _const_spec((3, 2 * D_FF)), _const_spec((1, 2 * D_FF)),
                  _const_spec((D_FF, D)), _const_spec((1, D))],
        out_specs=pl.BlockSpec((TM, D), lambda i: (i, 0)),
        scratch_shapes=[pltpu.VMEM((TM, D), BF16), pltpu.VMEM((TM, D_FF), BF16),
                        pltpu.VMEM((TM, FB), F32), pltpu.VMEM((TM, FB), F32)],
        compiler_params=_params(("arbitrary",)),
        name="conv_ffn",
    )(x, mod, g, up, cw, cb, down, fg)


def _head_sum(x, e_ref):
    return _dot(x.astype(BF16), e_ref[...])


def _swap16(x):
    lane = lax.broadcasted_iota(jnp.int32, x.shape, 1)
    n = x.shape[1]
    return jnp.where((lane & 16) == 0, pltpu.roll(x, n - 16, 1), pltpu.roll(x, 16, 1))


def _pre_cd_body(x_ref, xp_ref, xn_ref, mod_ref, g_ref, win_ref, lw1_ref, lw2_ref, la1_ref, la2_ref,
                 lg1_ref, lg2_ref, mu_rkv_ref, mu_wag_ref, w0_ref, a0_ref, kk_ref, qg_ref, kg_ref,
                 e_ref, cos_ref, sin_ref,
                 r_ref, k_ref, v_ref, kkn_ref, lwf_ref, lwb_ref, af_ref, ab_ref, gg_ref,
                 q_ref, kd_ref, vd_ref, ck_ref, cv_ref):
    i = pl.program_id(0)
    is_lat = i >= NS_CTX
    shift = mod_ref[0, 0:1, :]
    scale = mod_ref[0, 1:2, :]
    gn = g_ref[...]

    def adaln(x):
        return _rms(x) * gn * (1.0 + scale) + shift

    h = adaln(x_ref[...])
    hh = adaln(jnp.concatenate([xp_ref[...], xn_ref[...]], axis=0))
    hb = h.astype(BF16)

    seq = jnp.where(is_lat, L_LAT, L_CTX)

    def delta(z, zh):
        n = z.shape[1]
        r2 = lax.broadcasted_iota(jnp.int32, (TS, n), 0)
        rin2 = (r2 + i * TS) & (seq - 1)
        zp = jnp.where(r2 == 0, zh[7:8, :], pltpu.roll(z, 1, 0))
        zn = jnp.where(r2 == TS - 1, zh[8:9, :], pltpu.roll(z, TS - 1, 0))
        zp = jnp.where(rin2 != 0, zp, 0.0)
        zn = jnp.where(rin2 != seq - 1, zn, 0.0)
        return 0.5 * (zp + zn) - z

    z = _dot(hb, win_ref[:, 0:3 * HALF])
    zh = _dot(hh.astype(BF16), win_ref[:, 0:3 * HALF])
    outs = []
    for c in range(3):
        zc = z[:, c * HALF:(c + 1) * HALF]
        zhc = zh[:, c * HALF:(c + 1) * HALF]
        outs.append(zc + delta(zc, zhc) * mu_rkv_ref[c:c + 1, :])
    r, k, v = outs
    r_ref[...] = r
    k_ref[...] = k
    v_ref[...] = v
    kk = k * kk_ref[...]
    kkn_ref[...] = kk * lax.rsqrt(_head_sum(kk * kk, e_ref) + 1e-12)

    dh = delta(h, hh)
    xw = (h + dh * mu_wag_ref[0:1, :]).astype(BF16)
    xa = (h + dh * mu_wag_ref[1:2, :]).astype(BF16)
    xg = (h + dh * mu_wag_ref[2:3, :]).astype(BF16)
    tw = jnp.tanh(_dot(xw, lw1_ref[...])).astype(BF16)
    ta = _dot(xa, la1_ref[...]).astype(BF16)
    for d, (lw_out, a_out) in enumerate(((lwf_ref, af_ref), (lwb_ref, ab_ref))):
        wl = w0_ref[d:d + 1, :] + _dot(tw[:, d * 64:(d + 1) * 64], lw2_ref[d])
        lw_out[...] = (-math.exp(-0.5)) * _sigmoid(wl)
        a_out[...] = _sigmoid(a0_ref[d:d + 1, :] + _dot(ta[:, d * 64:(d + 1) * 64], la2_ref[d]))
    gg_ref[...] = _dot(_sigmoid(_dot(xg, lg1_ref[...])).astype(BF16), lg2_ref[...])

    lat_f = is_lat.astype(F32)
    cos = cos_ref[...] * lat_f + (1.0 - lat_f)
    sin = sin_ref[...] * lat_f
    zq = _dot(hb, win_ref[:, 3 * HALF:4 * HALF])
    qn = zq * lax.rsqrt(_head_sum(zq * zq, e_ref) * (1.0 / HD) + EPS) * qg_ref[...]
    cos4 = jnp.concatenate([cos] * NPAIR, axis=1)
    sin4 = jnp.concatenate([sin] * NPAIR, axis=1)
    q_ref[...] = ((qn * cos4 + _swap16(qn) * sin4) * (HD ** -0.5)).astype(BF16)

    zk = _dot(hb, win_ref[:, 4 * HALF:4 * HALF + 256])
    kn = zk * lax.rsqrt(_head_sum(zk * zk, e_ref[0:256, 0:256]) * (1.0 / HD) + EPS) * kg_ref[...]
    lane = lax.broadcasted_iota(jnp.int32, (TS, 128), 1)
    ck_ref[...] = jnp.where(lane < HD, kn[:, 0:128], kn[:, 128:256])
    cos2 = jnp.concatenate([cos, cos], axis=1)
    sin2 = jnp.concatenate([sin, sin], axis=1)
    kd_ref[...] = (kn * cos2 + _swap16(kn) * sin2).astype(BF16)
    zv = _dot(hb, win_ref[:, 4 * HALF + 256:4 * HALF + 512])
    cv_ref[...] = jnp.where(lane < HD, zv[:, 0:128], zv[:, 128:256])
    vd_ref[...] = zv.astype(BF16)


def _pre_cd(x, mod, g, win, lw1, lw2, la1, la2, lg1, lg2, mu_rkv, mu_wag, w0, a0, k_k, qg, kg, e, cos, sin):
    mi = _mod_index(NS_CTX)
    nblk8 = ROWS // 8
    row512 = lambda i: (i, 0)
    f32_out = jax.ShapeDtypeStruct((ROWS, HALF), F32)
    out_shape = [f32_out] * 9 + [jax.ShapeDtypeStruct((ROWS, HALF), BF16),
                                 jax.ShapeDtypeStruct((ROWS, 256), BF16),
                                 jax.ShapeDtypeStruct((ROWS, 256), BF16),
                                 jax.ShapeDtypeStruct((ROWS, 128), F32),
                                 jax.ShapeDtypeStruct((ROWS, 128), F32)]
    out_specs = [pl.BlockSpec((TS, HALF), row512)] * 10 + [pl.BlockSpec((TS, 256), row512)] * 2 + \
                [pl.BlockSpec((TS, 128), row512)] * 2
    return pl.pallas_call(
        _pre_cd_body,
        out_shape=out_shape,
        grid=(NS,),
        in_specs=[pl.BlockSpec((TS, D), row512),
                  pl.BlockSpec((8, D), lambda i: (jnp.maximum(i * (TS // 8) - 1, 0), 0)),
                  pl.BlockSpec((8, D), lambda i: (jnp.minimum((i + 1) * (TS // 8), nblk8 - 1), 0)),
                  pl.BlockSpec((1, 6, D), lambda i: (mi(i, TS), 0, 0)),
                  _const_spec((1, D)), _const_spec(win.shape),
                  _const_spec(lw1.shape), _const_spec(lw2.shape), _const_spec(la1.shape), _const_spec(la2.shape),
                  _const_spec(lg1.shape), _const_spec(lg2.shape),
                  _const_spec((3, HALF)), _const_spec((3, D)), _const_spec((2, HALF)), _const_spec((2, HALF)),
                  _const_spec((1, HALF)), _const_spec((1, HALF)), _const_spec((1, 256)),
                  _const_spec((HALF, HALF)),
                  pl.BlockSpec((TS, 128), lambda i: (i % (L_LAT // TS), 0)),
                  pl.BlockSpec((TS, 128), lambda i: (i % (L_LAT // TS), 0))],
        out_specs=out_specs,
        compiler_params=_params(("arbitrary",)),
        name="pre_cd",
    )(x, x, x, mod, g, win, lw1, lw2, la1, la2, lg1, lg2, mu_rkv, mu_wag, w0, a0, k_k, qg, kg, e, cos, sin)


def _scan_unit(rc, kc, vc, kkc, lwc, ac, ka, s_prev, reverse, masks):
    cum_mask, tri_s, tri_i, eye, lane_lo, diag8, offs = masks
    cum = jnp.dot(cum_mask, lwc, preferred_element_type=F32, precision=lax.Precision.HIGHEST)
    cume = cum - lwc
    if reverse:
        tot, mid = cum[0:1, :], cum[SC // 2:SC // 2 + 1, :]
    else:
        tot, mid = cum[SC - 1:SC, :], cum[SC // 2 - 1:SC // 2, :]
    at = -kkc * jnp.exp(cume - mid)
    rt = rc * jnp.exp(cum - mid)
    e3 = jnp.exp(mid - cum)
    kt = kc * (1.0 + (ac - 1.0) * ka) * e3
    bt = kkc * ac * e3

    def stack(x):
        return jnp.concatenate([jnp.where(lane_lo, x, 0.0), jnp.where(lane_lo, 0.0, x)], axis=0)

    a2 = stack(at)
    r2 = stack(rt)
    v2 = stack(vc).astype(BF16)
    kb = jnp.concatenate([stack(kt), stack(bt)], axis=0).astype(BF16)
    lhs = jnp.concatenate([a2, r2], axis=0).astype(BF16)
    rhs = jnp.concatenate([bt, bt, kt, kt], axis=0).astype(BF16)
    yield
    p = _dot_nt(lhs, rhs)
    a_ab = jnp.where(tri_s, p[0:128, 0:128], 0.0)
    a_ak = jnp.where(tri_s, p[0:128, 128:256], 0.0)
    a_rb = jnp.where(tri_i, p[128:256, 0:128], 0.0)
    a_rk = jnp.where(tri_i, p[128:256, 128:256], 0.0)

    lk = jnp.where(diag8, a_ab, 0.0)
    tinv = eye + lk
    for _ in range(2):
        lkb = lk.astype(BF16)
        yield
        lk = _dot(lkb, lkb)
        yield
        tinv = _dot(tinv.astype(BF16), (eye + lk).astype(BF16))
    for off in offs:
        tb = tinv.astype(BF16)
        yield
        tl = _dot(tb, jnp.where(off, a_ab, 0.0).astype(BF16)).astype(BF16)
        yield
        tinv = tinv + _dot(tl, tb)

    akv = _dot(a_ak.astype(BF16), v2)
    yield
    wu = _dot(tinv.astype(BF16), jnp.concatenate([a2, akv], axis=1).astype(BF16))
    q = jnp.concatenate([jnp.concatenate([jnp.zeros((128, 128), BF16), v2], axis=1),
                         wu.astype(BF16)], axis=0)
    yield
    ry = _dot(jnp.concatenate([a_rk, a_rb], axis=1).astype(BF16), q)
    rw2 = r2 + ry[:, 0:128]
    y02 = ry[:, 128:256]
    gh = _dot_tn(q, kb)

    sd = s_prev * jnp.exp(mid)
    sdb = sd.astype(BF16)
    yield
    ys = _dot_nt(rw2.astype(BF16), sdb)
    sg = _dot(sdb, gh[0:128, :].astype(BF16))
    s_new = s_prev * jnp.exp(tot) + (sg + gh[128:256, :]) * jnp.exp(tot - mid)
    y2 = y02 + ys
    return y2[0:SC, :] + y2[SC:2 * SC, :], s_new


def _run_lockstep(gens):
    results = [None] * len(gens)
    live = list(range(len(gens)))
    while live:
        still = []
        for u in live:
            try:
                next(gens[u])
                still.append(u)
            except StopIteration as stop:
                results[u] = stop.value
        live = still
    return results


def _scan_body(tab_ref, rf_ref, kf_ref, vf_ref, kkf_ref, lwf_ref, af_ref,
               rb_ref, kb_ref, vb_ref, kkb_ref, lwb_ref, ab_ref, ka_ref, s0f_ref, s0b_ref,
               yf_ref, yb_ref, sf_ref, sb_ref, st_ref):
    t = pl.program_id(0)
    first = tab_ref[2, t]
    last = tab_ref[3, t]
    is_lat = tab_ref[4, t]

    @pl.when(first == 1)
    def _():
        zero = jnp.zeros((HD, HD), F32)
        for d, s0_ref in enumerate((s0f_ref, s0b_ref)):
            for p in range(NPAIR):
                lo = s0_ref[0, 2 * p] * is_lat.astype(F32)
                hi = s0_ref[0, 2 * p + 1] * is_lat.astype(F32)
                st_ref[d * NPAIR + p] = jnp.concatenate(
                    [jnp.concatenate([lo, zero], axis=1), jnp.concatenate([zero, hi], axis=1)], axis=0)

    ri = lax.broadcasted_iota(jnp.int32, (SC, SC), 0)
    ci = lax.broadcasted_iota(jnp.int32, (SC, SC), 1)
    r2 = lax.broadcasted_iota(jnp.int32, (128, 128), 0)
    c2 = lax.broadcasted_iota(jnp.int32, (128, 128), 1)
    same = (r2 & SC) == (c2 & SC)
    eye = (r2 == c2).astype(F32)
    diag8 = (r2 >> 3) == (c2 >> 3)
    lane_lo = lax.broadcasted_iota(jnp.int32, (SC, 128), 1) < HD

    dirs = ((False, (rf_ref, kf_ref, vf_ref, kkf_ref, lwf_ref, af_ref), yf_ref),
            (True, (rb_ref, kb_ref, vb_ref, kkb_ref, lwb_ref, ab_ref), yb_ref))
    gens, dests = [], []
    for d, (reverse, refs, y_ref) in enumerate(dirs):
        early, late = (r2, c2) if reverse else (c2, r2)
        offs = tuple(((r2 >> sh + 1) == (c2 >> sh + 1)) & ((late & (1 << sh)) != 0) & ((early & (1 << sh)) == 0)
                     for sh in (3, 4, 5))
        if reverse:
            masks = ((ri <= ci).astype(F32), same & (r2 < c2), same & (r2 <= c2), eye, lane_lo, diag8, offs)
        else:
            masks = ((ri >= ci).astype(F32), same & (r2 > c2), same & (r2 >= c2), eye, lane_lo, diag8, offs)
        for p in range(NPAIR):
            ls = slice(p * 128, (p + 1) * 128)
            rc, kc, vc, kkc, lwc, ac = (ref[:, ls] for ref in refs)
            gens.append(_scan_unit(rc, kc, vc, kkc, lwc, ac, ka_ref[:, ls], st_ref[d * NPAIR + p],
                                   reverse, masks))
            dests.append((y_ref, ls, d * NPAIR + p))
    for (y_ref, ls, u), (y, s_new) in zip(dests, _run_lockstep(gens)):
        y_ref[:, ls] = y
        st_ref[u] = s_new

    @pl.when(last == 1)
    def _():
        for d, s_out in enumerate((sf_ref, sb_ref)):
            for p in range(NPAIR):
                s = st_ref[d * NPAIR + p]
                s_out[0, 2 * p] = s[0:HD, 0:HD]
                s_out[0, 2 * p + 1] = s[HD:2 * HD, HD:2 * HD]


def _scan_tables():
    fwd, bwd, first, last, lat, seq_ctx, seq_lat = [], [], [], [], [], [], []
    for group, (nb, ln, base) in enumerate(((N_CTX, L_CTX, 0), (N_LAT, L_LAT, ROWS_CTX))):
        nc = ln // SC
        for b in range(nb):
            for c in range(nc):
                blk = (base + b * ln) // SC
                fwd.append(blk + c)
                bwd.append(blk + nc - 1 - c)
                first.append(int(c == 0))
                last.append(int(c == nc - 1))
                lat.append(group)
                seq_ctx.append(b if group == 0 else N_CTX)
                seq_lat.append(b if group == 1 else 0)
    return np.asarray([fwd, bwd, first, last, lat, seq_ctx, seq_lat], np.int32)


def _scan(r, k, v, kk, lwf, lwb, af, ab, k_a, s0f, s0b):
    tab = _scan_tables()
    steps = tab.shape[1]
    fspec = pl.BlockSpec((SC, HALF), lambda t, tab: (tab[0, t], 0))
    bspec = pl.BlockSpec((SC, HALF), lambda t, tab: (tab[1, t], 0))
    s_in = pl.BlockSpec((1, 2 * NPAIR, HD, HD), lambda t, tab: (tab[6, t], 0, 0, 0))
    s_out = pl.BlockSpec((1, 2 * NPAIR, HD, HD), lambda t, tab: (tab[5, t], 0, 0, 0))
    y_shape = jax.ShapeDtypeStruct((ROWS, HALF), F32)
    s_shape = jax.ShapeDtypeStruct((N_CTX + 1, 2 * NPAIR, HD, HD), F32)
    grid_spec = pltpu.PrefetchScalarGridSpec(
        num_scalar_prefetch=1,
        grid=(steps,),
        in_specs=[fspec] * 6 + [bspec] * 6 + [pl.BlockSpec((1, HALF), lambda t, tab: (0, 0)), s_in, s_in],
        out_specs=[fspec, bspec, s_out, s_out],
        scratch_shapes=[pltpu.VMEM((2 * NPAIR, 128, 128), F32)],
    )
    return pl.pallas_call(
        _scan_body,
        out_shape=[y_shape, y_shape, s_shape, s_shape],
        grid_spec=grid_spec,
        compiler_params=_params(("arbitrary",)),
        name="wkv_scan",
    )(jnp.asarray(tab), r, k, v, kk, lwf, af, r, k, v, kk, lwb, ab, k_a, s0f, s0b)


def _attn_body(q_ref, k_ref, v_ref, *rest, cached):
    if cached:
        ck_ref, cv_ref, o_ref = rest
    else:
        (o_ref,) = rest
    lane_lo = lax.broadcasted_iota(jnp.int32, (QB, 128), 1) < HD
    for pr in range(NPAIR):
        qp = q_ref[:, pr * 128:(pr + 1) * 128]
        kvh = (2 * pr) // 4
        ks = slice(kvh * 128, (kvh + 1) * 128)
        kd = k_ref[:, ks]
        vd = v_ref[:, ks]
        halves = []
        for half in range(2):
            qh = jnp.where(lane_lo if half == 0 else jnp.logical_not(lane_lo), qp, jnp.zeros_like(qp))
            s_new = _dot_nt(qh, kd)
            if cached:
                s_old = _dot_nt(qh, ck_ref[0, :, ks])
                m = jnp.maximum(jnp.max(s_new, -1, keepdims=True), jnp.max(s_old, -1, keepdims=True))
                p_new = jnp.exp(s_new - m)
                p_old = jnp.exp(s_old - m)
                den = jnp.sum(p_new, -1, keepdims=True) + jnp.sum(p_old, -1, keepdims=True)
                o = _dot(p_new.astype(BF16), vd) + _dot(p_old.astype(BF16), cv_ref[0, :, ks])
            else:
                m = jnp.max(s_new, -1, keepdims=True)
                p_new = jnp.exp(s_new - m)
                den = jnp.sum(p_new, -1, keepdims=True)
                o = _dot(p_new.astype(BF16), vd)
            halves.append(o / den)
        o_ref[:, pr * 128:(pr + 1) * 128] = jnp.where(lane_lo, halves[0], halves[1]).astype(BF16)


def _attention(q, kd, vd, base, nb, ln, cache=None):
    nq = ln // QB
    blk0 = base // QB
    seq0 = base // ln
    in_specs = [pl.BlockSpec((QB, HALF), lambda b, j: (blk0 + b * nq + j, 0)),
                pl.BlockSpec((ln, 256), lambda b, j: (seq0 + b, 0)),
                pl.BlockSpec((ln, 256), lambda b, j: (seq0 + b, 0))]
    args = [q, kd, vd]
    if cache is not None:
        in_specs += [pl.BlockSpec((1, PAST, 256), lambda b, j: (b, 0, 0))] * 2
        args += list(cache)
    return pl.pallas_call(
        functools.partial(_attn_body, cached=cache is not None),
        out_shape=jax.ShapeDtypeStruct((nb * ln, HALF), BF16),
        grid=(nb, nq),
        in_specs=in_specs,
        out_specs=pl.BlockSpec((QB, HALF), lambda b, j: (b * nq + j, 0)),
        compiler_params=_params(("arbitrary", "arbitrary")),
        name="gqa_cached" if cache is not None else "gqa",
    )(*args)


def _post_cd_body(x_ref, mod_ref, yf_ref, yb_ref, r_ref, k_ref, v_ref, af_ref, ab_ref, gg_ref,
                  atc_ref, atl_ref, wout_ref, e_ref, lng_ref, lnb_ref, rk_ref, ka_ref, o_ref):
    i = pl.program_id(0)
    gate = mod_ref[0, 2:3, :]
    y = yf_ref[...] + yb_ref[...]
    yc = y - _head_sum(y, e_ref) * (1.0 / HD)
    yn = yc * lax.rsqrt(_head_sum(yc * yc, e_ref) * (1.0 / HD) + GN_EPS) * lng_ref[...] + lnb_ref[...]
    kd2 = k_ref[...] * (2.0 + (af_ref[...] + ab_ref[...] - 2.0) * ka_ref[...])
    bonus = _head_sum(r_ref[...] * kd2 * rk_ref[...], e_ref) * v_ref[...]
    yc_out = ((yn + bonus) * gg_ref[...]).astype(BF16)
    att = jnp.where(i >= NS_CTX, atl_ref[...], atc_ref[...])
    out = _dot(yc_out, wout_ref[0:HALF, :]) + _dot(att, wout_ref[HALF:2 * HALF, :])
    o_ref[...] = x_ref[...] + gate * out


def _post_cd(x, mod, yf, yb, r, k, v, af, ab, gg, att_c, att_l, wout, e, lng, lnb, r_k, k_a):
    mi = _mod_index(NS_CTX)
    row = lambda i: (i, 0)
    half = pl.BlockSpec((TS, HALF), row)
    return pl.pallas_call(
        _post_cd_body,
        out_shape=jax.ShapeDtypeStruct((ROWS, D), F32),
        grid=(NS,),
        in_specs=[pl.BlockSpec((TS, D), row),
                  pl.BlockSpec((1, 6, D), lambda i: (mi(i, TS), 0, 0))] + [half] * 8 +
                 [pl.BlockSpec((TS, HALF), lambda i: (jnp.minimum(i, NS_CTX - 1), 0)),
                  pl.BlockSpec((TS, HALF), lambda i: (jnp.maximum(i - NS_CTX, 0), 0)),
                  _const_spec((D, D)), _const_spec((HALF, HALF)),
                  _const_spec((1, HALF)), _const_spec((1, HALF)), _const_spec((1, HALF)), _const_spec((1, HALF))],
        out_specs=pl.BlockSpec((TS, D), row),
        compiler_params=_params(("arbitrary",)),
        name="post_cd",
    )(x, mod, yf, yb, r, k, v, af, ab, gg, att_c, att_l, wout, e, lng, lnb, r_k, k_a)


def _rope_tables():
    half = 16
    inv = 10000.0 ** (-jnp.arange(half, dtype=F32) / half)
    t = jnp.arange(L_LAT)
    cos_parts, sin_parts = [], []
    for pos in (t // GRID_W, t % GRID_W):
        ang = pos.astype(F32)[:, None] * inv[None, :]
        cos_parts += [jnp.cos(ang), jnp.cos(ang)]
        sin_parts += [-jnp.sin(ang), jnp.sin(ang)]
    cos = jnp.concatenate(cos_parts * 2, axis=1)
    sin = jnp.concatenate(sin_parts * 2, axis=1)
    return cos, sin


def kernel(x_prompt, x_sample, cache_k, cache_v, state_wkv_fwd, state_wkv_bwd, c, c_ctx, mod_w, mod_b, norm1_g, norm2_g, w_out, ffn_up, ffn_conv_w, ffn_conv_b, ffn_down, final_g, ab_w_in, a_vnorm_g, a_vnorm_b, a_ws, a_bs, b_conv_w, b_conv_b, b_norm_g, cd_w_in, c_mu_rkv, c_mu_wag, c_w0, c_w1, c_w2, c_a0, c_a1, c_a2, c_g1, c_g2, c_k_k, c_k_a, c_r_k, c_ln_g, c_ln_b, d_q_g, d_k_g):
    row = lambda a: a.reshape(1, -1)
    cvec =jnp.concatenate([c_ctx[None, :], c, jnp.zeros((16 - 1 - N_LAT, D), F32)], axis=0)
    mods = _modulation(cvec, mod_w, mod_b).reshape(2, 16, 6, D)

    bs = jnp.broadcast_to(a_bs[0][:, :, None], (A_GROUPS, CHUNK, 128))
    cw = jnp.concatenate([b_conv_w[0], jnp.zeros((1, HALF), F32)], axis=0)
    x = _mix_ab(x_prompt.reshape(ROWS_CTX, D), x_sample.reshape(ROWS_LAT, D), mods[0], row(norm1_g[0]),
                ab_w_in[0].astype(BF16), w_out[0].astype(BF16),
                a_ws[0].astype(BF16), bs, row(a_vnorm_g[0]), row(a_vnorm_b[0]),
                cw, row(b_conv_b[0]), row(b_norm_g[0]))
    x = _ffn(x, 0, NT, mods[0], row(norm2_g[0]), ffn_up[0].astype(BF16), ffn_conv_w[0], row(ffn_conv_b[0]),
             ffn_down[0].astype(BF16), row(final_g), final=False)

    w = cd_w_in[0]
    wk = w[:, 4 * HALF:4 * HALF + 128]
    wv = w[:, 4 * HALF + 128:4 * HALF + 256]
    win = jnp.concatenate([w[:, :4 * HALF], wk[:, :HD], wk[:, :HD], wk[:, HD:], wk[:, HD:],
                           wv[:, :HD], wv[:, :HD], wv[:, HD:], wv[:, HD:]], axis=1).astype(BF16)
    lw1 = jnp.concatenate([c_w1[0, 0], c_w1[0, 1]], axis=1).astype(BF16)
    la1 = jnp.concatenate([c_a1[0, 0], c_a1[0, 1]], axis=1).astype(BF16)
    ones_blk = jnp.asarray(np.kron(np.eye(HALF // HD), np.ones((HD, HD))), BF16)
    cos, sin = _rope_tables()
    k_a = row(c_k_a[0])
    (r, k, v, kk, lwf, lwb, af, ab, gg, q, kd, vd, ck, cv) = _pre_cd(
        x, mods[1], row(norm1_g[1]), win, lw1, c_w2[0].astype(BF16), la1, c_a2[0].astype(BF16),
        c_g1[0].astype(BF16), c_g2[0].astype(BF16), c_mu_rkv[0], c_mu_wag[0], c_w0[0], c_a0[0],
        row(c_k_k[0]), jnp.tile(d_q_g[0], HALF // HD)[None, :], jnp.tile(d_k_g[0], 4)[None, :],
        ones_blk, cos, sin)

    yf, yb, s_f, s_b = _scan(r, k, v, kk, lwf, lwb, af, ab, k_a, state_wkv_fwd[:, 0], state_wkv_bwd[:, 0])

    def dup_cache(t):
        t = t[:, 0].astype(BF16)
        return jnp.concatenate([t[:, :, 0], t[:, :, 0], t[:, :, 1], t[:, :, 1]], axis=-1)

    att_c = _attention(q, kd, vd, 0, N_CTX, L_CTX)
    att_l = _attention(q, kd, vd, ROWS_CTX, N_LAT, L_LAT, cache=(dup_cache(cache_k), dup_cache(cache_v)))

    x = _post_cd(x, mods[1], yf, yb, r, k, v, af, ab, gg, att_c, att_l, w_out[1].astype(BF16), ones_blk,
                 row(c_ln_g[0]), row(c_ln_b[0]), row(c_r_k[0]), k_a)
    ffn1 = (mods[1], row(norm2_g[1]), ffn_up[1].astype(BF16), ffn_conv_w[1], row(ffn_conv_b[1]),
            ffn_down[1].astype(BF16), row(final_g))
    y_prompt = _ffn(x, 0, NT_CTX, *ffn1, final=True).reshape(N_CTX, L_CTX, D)
    y_sample = _ffn(x, NT_CTX, NT - NT_CTX, *ffn1, final=True).reshape(N_LAT, L_LAT, D)
    new_k = ck[:ROWS_CTX].reshape(N_CTX, 1, L_CTX, 2, HD)
    new_v = cv[:ROWS_CTX].reshape(N_CTX, 1, L_CTX, 2, HD)
    return (y_prompt, y_sample, new_k, new_v,
            s_f[:N_CTX].reshape(N_CTX, 1, 2 * NPAIR, HD, HD), s_b[:N_CTX].reshape(N_CTX, 1, 2 * NPAIR, HD, HD))
```

```python
import functools
import math

import jax
import jax.numpy as jnp
import numpy as np
from jax import lax
from jax.experimental import pallas as pl
from jax.experimental.pallas import tpu as pltpu

F32 = jnp.float32
BF16 = jnp.bfloat16

D = 1024
N_CTX, L_CTX = 16, 256
N_LAT, L_LAT = 8, 1024
ROWS_CTX = N_CTX * L_CTX
ROWS_LAT = N_LAT * L_LAT
ROWS = ROWS_CTX + ROWS_LAT
PAST = 256
GRID_W = 64
EPS = 1e-6
GN_EPS = 64e-5
HALF = 512
HD = 64
NPAIR = HALF // 128
D_FF = 2816
B_CONV = 31
CHUNK = 128
A_GROUPS = 4

TM = 1024
NT = ROWS // TM
NT_CTX = ROWS_CTX // TM
SEG = 256
NSEG = TM // SEG
HALO = 16
TS = 512
NS = ROWS // TS
NS_CTX = ROWS_CTX // TS
FB = 256
NJ = D_FF // FB
SC = 64
NMEM = 2
QB = 256
VMEM_LIMIT = 56 * 1024 * 1024

_NT_DIMS = (((1,), (1,)), ((), ()))
_TN_DIMS = (((0,), (0,)), ((), ()))


def _dot(a, b):
    return jnp.dot(a, b, preferred_element_type=F32)


def _dot_nt(a, b):
    return lax.dot_general(a, b, _NT_DIMS, preferred_element_type=F32)


def _dot_tn(a, b):
    return lax.dot_general(a, b, _TN_DIMS, preferred_element_type=F32)


def _rms(x, eps=EPS):
    return x * lax.rsqrt(jnp.mean(x * x, -1, keepdims=True) + eps)


def _sigmoid(x):
    return 1.0 / (1.0 + jnp.exp(-x))


def _silu(x):
    return x * _sigmoid(x)


def _gelu_tanh(x):
    return 0.5 * x * (1.0 + jnp.tanh(math.sqrt(2.0 / math.pi) * (x + 0.044715 * (x * x * x))))


def _const_spec(shape):
    return pl.BlockSpec(shape, lambda *_: (0,) * len(shape), pipeline_mode=pl.Buffered(1))


def _params(sem):
    return pltpu.CompilerParams(dimension_semantics=sem, vmem_limit_bytes=VMEM_LIMIT)


def _mod_body(c_ref, w_ref, b_ref, o_ref):
    s = _silu(c_ref[...])
    o_ref[0] = jnp.dot(s, w_ref[0], preferred_element_type=F32,
                       precision=lax.Precision.HIGHEST) + b_ref[0]


def _modulation(cvec, mod_w, mod_b):
    depth = mod_w.shape[0]
    nb = 1536
    return pl.pallas_call(
        _mod_body,
        out_shape=jax.ShapeDtypeStruct((depth, 16, 6 * D), F32),
        grid=(depth, 6 * D // nb),
        in_specs=[pl.BlockSpec((16, D), lambda l, j: (0, 0)),
                  pl.BlockSpec((1, D, nb), lambda l, j: (l, 0, j)),
                  pl.BlockSpec((1, 1, nb), lambda l, j: (l, 0, j))],
        out_specs=pl.BlockSpec((1, 16, nb), lambda l, j: (l, 0, j)),
        compiler_params=_params(("arbitrary", "arbitrary")),
        name="modulation",
    )(cvec, mod_w, mod_b.reshape(depth, 1, 6 * D))


def _mod_index(tiles_ctx):
    def idx(i, tile_rows):
        return jnp.where(i >= tiles_ctx, (i - tiles_ctx) // (L_LAT // tile_rows) + 1, 0)
    return idx


def _mix_ab_body(xc_ref, xl_ref, mod_ref, g_ref, win_ref, wout_ref, ws_ref, bs_ref, vg_ref, vb_ref,
                 cw_ref, cb_ref, ng_ref, o_ref, cat_ref, pad_ref, shift_ref):
    i = pl.program_id(0)
    is_lat = i >= NT_CTX
    lat = is_lat.astype(F32)

    def x_rows(r0):
        return jnp.where(is_lat, xl_ref[pl.ds(r0, SEG), :], xc_ref[pl.ds(r0, SEG), :])

    shift = mod_ref[0, 0:1, :]
    scale = mod_ref[0, 1:2, :]
    gate = mod_ref[0, 2:3, :]

    def phase1(s, carry):
        r0 = pl.multiple_of(s * SEG, SEG)
        x = x_rows(r0)
        h = (_rms(x) * g_ref[...] * (1.0 + scale) + shift).astype(BF16)
        z = _dot(h, win_ref[...])
        za = _gelu_tanh(z[:, :2 * HALF])
        u = za[:, :HALF]
        v = za[:, HALF:]
        vc = v - jnp.mean(v, -1, keepdims=True)
        vn = vc * lax.rsqrt(jnp.mean(vc * vc, -1, keepdims=True) + EPS) * vg_ref[...] + vb_ref[...]
        vnb = vn.astype(BF16)
        for n in range(SEG // CHUNK):
            parts = []
            for g in range(A_GROUPS):
                vv = vnb[n * CHUNK:(n + 1) * CHUNK, g * 128:(g + 1) * 128]
                parts.append(_dot(ws_ref[g], vv) + bs_ref[g])
            vp = jnp.concatenate(parts, axis=1)
            ya = u[n * CHUNK:(n + 1) * CHUNK, :] * vp
            cat_ref[pl.ds(pl.multiple_of(r0 + n * CHUNK, CHUNK), CHUNK), 0:HALF] = ya.astype(BF16)
        zb = z[:, 2 * HALF:]
        pad_ref[s, HALO:HALO + SEG, :] = zb[:, :HALF] * _sigmoid(zb[:, HALF:])
        return carry

    lax.fori_loop(0, NSEG, phase1, 0)

    zeros_h = jnp.zeros((HALO, HALF), F32)
    for s in range(NSEG):
        left = pad_ref[s - 1, SEG:SEG + HALO, :] * lat if s > 0 else zeros_h
        right = pad_ref[s + 1, HALO:2 * HALO, :] * lat if s < NSEG - 1 else zeros_h
        pad_ref[s, 0:HALO, :] = left
        pad_ref[s, HALO + SEG:2 * HALO + SEG, :] = right

    RB = 32

    shift_rows = SEG + 2 * HALO - 8

    def phase3(s, carry):
        r0 = pl.multiple_of(s * SEG, SEG)
        for sh in range(1, 8):
            shift_ref[sh - 1] = pad_ref[s, sh:sh + shift_rows, :]
        for rb in range(SEG // RB):
            acc = jnp.zeros((RB, HALF), F32) + cb_ref[...]
            for k in range(B_CONV):
                q8, sh = divmod(HALO - B_CONV // 2 + k, 8)
                off = 8 * q8 + rb * RB
                win = pad_ref[s, off:off + RB, :] if sh == 0 else shift_ref[sh - 1, off:off + RB, :]
                acc = acc + cw_ref[k:k + 1, :] * win
            yb = _silu(_rms(acc) * ng_ref[...])
            cat_ref[pl.ds(pl.multiple_of(r0 + rb * RB, RB), RB), HALF:2 * HALF] = yb.astype(BF16)
        out = _dot(cat_ref[pl.ds(r0, SEG), :], wout_ref[...])
        o_ref[pl.ds(r0, SEG), :] = x_rows(r0) + gate * out
        return carry

    lax.fori_loop(0, NSEG, phase3, 0)


def _mix_ab(xc, xl, mod, g, win, wout, ws, bs, vg, vb, cw, cb, ng):
    mi = _mod_index(NT_CTX)
    return pl.pallas_call(
        _mix_ab_body,
        out_shape=jax.ShapeDtypeStruct((ROWS, D), F32),
        grid=(NT,),
        in_specs=[pl.BlockSpec((TM, D), lambda i: (jnp.minimum(i, NT_CTX - 1), 0)),
                  pl.BlockSpec((TM, D), lambda i: (jnp.maximum(i - NT_CTX, 0), 0)),
                  pl.BlockSpec((1, 6, D), lambda i: (mi(i, TM), 0, 0)),
                  _const_spec((1, D)), _const_spec((D, 4 * HALF)), _const_spec((D, D)),
                  _const_spec((A_GROUPS, CHUNK, CHUNK)), _const_spec((A_GROUPS, CHUNK, 128)),
                  _const_spec((1, HALF)), _const_spec((1, HALF)),
                  _const_spec((B_CONV + 1, HALF)), _const_spec((1, HALF)), _const_spec((1, HALF))],
        out_specs=pl.BlockSpec((TM, D), lambda i: (i, 0)),
        scratch_shapes=[pltpu.VMEM((TM, D), BF16),
                        pltpu.VMEM((NSEG, SEG + 2 * HALO, HALF), F32),
                        pltpu.VMEM((7, SEG + 2 * HALO - 8, HALF), F32)],
        compiler_params=_params(("arbitrary",)),
        name="mix_ab",
    )(xc, xl, mod, g, win, wout, ws, bs, vg, vb, cw, cb, ng)


def _ffn_body(x_ref, mod_ref, g_ref, up_ref, cw_ref, cb_ref, down_ref, fg_ref, o_ref,
              h_ref, gated_ref, cza_ref, czb_ref, *, tile0, final):
    is_ctx = tile0 + pl.program_id(0) < NT_CTX
    shift = mod_ref[0, 3:4, :]
    scale = mod_ref[0, 4:5, :]
    h_ref[...] = (_rms(x_ref[...]) * g_ref[...] * (1.0 + scale) + shift).astype(BF16)

    def up_proj(col):
        return _dot(h_ref[...], up_ref[:, col * FB:(col + 1) * FB])

    def conv_branch(col, z, cz_ref):
        cs = slice(col * FB, (col + 1) * FB)
        w0, w1, w2, b = cw_ref[0:1, cs], cw_ref[1:2, cs], cw_ref[2:3, cs], cb_ref[:, cs]
        cz_ref[...] = w0 * pltpu.roll(z, 1, 0) + w1 * z + w2 * pltpu.roll(z, TM - 1, 0) + b
        for t0 in range(0, TM, L_CTX):
            t1 = t0 + L_CTX - 1
            first = w1 * z[t0:t0 + 1, :] + w2 * z[t0 + 1:t0 + 2, :] + b
            last = w0 * z[t1 - 1:t1, :] + w1 * z[t1:t1 + 1, :] + b
            cz_ref[t0:t0 + 1, :] = first if t0 == 0 else jnp.where(is_ctx, first, cz_ref[t0:t0 + 1, :])
            cz_ref[t1:t1 + 1, :] = last if t1 == TM - 1 else jnp.where(is_ctx, last, cz_ref[t1:t1 + 1, :])

    for j in range(NJ):
        conv_branch(j, up_proj(j), cza_ref)
        conv_branch(NJ + j, up_proj(NJ + j), czb_ref)
        gated_ref[:, j * FB:(j + 1) * FB] = (_silu(czb_ref[...]) * cza_ref[...]).astype(BF16)

    y = x_ref[...] + mod_ref[0, 5:6, :] * _dot(gated_ref[...], down_ref[...])
    if final:
        y = _rms(y) * fg_ref[...]
    o_ref[...] = y


def _ffn(x, tile0, ntiles, mod, g, up, cw, cb, down, fg, final):
    mi = _mod_index(NT_CTX)
    return pl.pallas_call(
        functools.partial(_ffn_body, tile0=tile0, final=final),
        out_shape=jax.ShapeDtypeStruct((ntiles * TM, D), F32),
        grid=(ntiles,),
        in_specs=[pl.BlockSpec((TM, D), lambda i: (tile0 + i, 0)),
                  pl.BlockSpec((1, 6, D), lambda i: (mi(tile0 + i, TM), 0, 0)),
                  _const_spec((1, D)),
                  _const_spec((D, 2 * D_FF)), _const_spec((3, 2 * D_FF)), _const_spec((1, 2 * D_FF)),
                  _const_spec((D_FF, D)), _const_spec((1, D))],
        out_specs=pl.BlockSpec((TM, D), lambda i: (i, 0)),
        scratch_shapes=[pltpu.VMEM((TM, D), BF16), pltpu.VMEM((TM, D_FF), BF16),
                        pltpu.VMEM((TM, FB), F32), pltpu.VMEM((TM, FB), F32)],
        compiler_params=_params(("arbitrary",)),
        name="conv_ffn",
    )(x, mod, g, up, cw, cb, down, fg)


def _head_sum(x, e_ref):
    return _dot(x.astype(BF16), e_ref[...])


def _swap16(x):
    lane = lax.broadcasted_iota(jnp.int32, x.shape, 1)
    n = x.shape[1]
    return jnp.where((lane & 16) == 0, pltpu.roll(x, n - 16, 1), pltpu.roll(x, 16, 1))


def _pre_cd_body(x_ref, xp_ref, xn_ref, mod_ref, g_ref, win_ref, lw1_ref, lw2_ref, la1_ref, la2_ref,
                 lg1_ref, lg2_ref, mu_rkv_ref, mu_wag_ref, w0_ref, a0_ref, kk_ref, qg_ref, kg_ref,
                 e_ref, cos_ref, sin_ref,
                 r_ref, k_ref, v_ref, kkn_ref, lwf_ref, lwb_ref, af_ref, ab_ref, gg_ref,
                 q_ref, kd_ref, vd_ref, ck_ref, cv_ref):
    i = pl.program_id(0)
    is_lat = i >= NS_CTX
    shift = mod_ref[0, 0:1, :]
    scale = mod_ref[0, 1:2, :]
    gn = g_ref[...]

    def adaln(x):
        return _rms(x) * gn * (1.0 + scale) + shift

    h = adaln(x_ref[...])
    hh = adaln(jnp.concatenate([xp_ref[...], xn_ref[...]], axis=0))
    hb = h.astype(BF16)

    seq = jnp.where(is_lat, L_LAT, L_CTX)

    def delta(z, zh):
        n = z.shape[1]
        r2 = lax.broadcasted_iota(jnp.int32, (TS, n), 0)
        rin2 = (r2 + i * TS) & (seq - 1)
        zp = jnp.where(r2 == 0, zh[7:8, :], pltpu.roll(z, 1, 0))
        zn = jnp.where(r2 == TS - 1, zh[8:9, :], pltpu.roll(z, TS - 1, 0))
        zp = jnp.where(rin2 != 0, zp, 0.0)
        zn = jnp.where(rin2 != seq - 1, zn, 0.0)
        return 0.5 * (zp + zn) - z

    z = _dot(hb, win_ref[:, 0:3 * HALF])
    zh = _dot(hh.astype(BF16), win_ref[:, 0:3 * HALF])
    outs = []
    for c in range(3):
        zc = z[:, c * HALF:(c + 1) * HALF]
        zhc = zh[:, c * HALF:(c + 1) * HALF]
        outs.append(zc + delta(zc, zhc) * mu_rkv_ref[c:c + 1, :])
    r, k, v = outs
    r_ref[...] = r
    k_ref[...] = k
    v_ref[...] = v
    kk = k * kk_ref[...]
    kkn_ref[...] = kk * lax.rsqrt(_head_sum(kk * kk, e_ref) + 1e-12)

    dh = delta(h, hh)
    xw = (h + dh * mu_wag_ref[0:1, :]).astype(BF16)
    xa = (h + dh * mu_wag_ref[1:2, :]).astype(BF16)
    xg = (h + dh * mu_wag_ref[2:3, :]).astype(BF16)
    tw = jnp.tanh(_dot(xw, lw1_ref[...])).astype(BF16)
    ta = _dot(xa, la1_ref[...]).astype(BF16)
    for d, (lw_out, a_out) in enumerate(((lwf_ref, af_ref), (lwb_ref, ab_ref))):
        wl = w0_ref[d:d + 1, :] + _dot(tw[:, d * 64:(d + 1) * 64], lw2_ref[d])
        lw_out[...] = (-math.exp(-0.5)) * _sigmoid(wl)
        a_out[...] = _sigmoid(a0_ref[d:d + 1, :] + _dot(ta[:, d * 64:(d + 1) * 64], la2_ref[d]))
    gg_ref[...] = _dot(_sigmoid(_dot(xg, lg1_ref[...])).astype(BF16), lg2_ref[...])

    lat_f = is_lat.astype(F32)
    cos = cos_ref[...] * lat_f + (1.0 - lat_f)
    sin = sin_ref[...] * lat_f
    zq = _dot(hb, win_ref[:, 3 * HALF:4 * HALF])
    qn = zq * lax.rsqrt(_head_sum(zq * zq, e_ref) * (1.0 / HD) + EPS) * qg_ref[...]
    cos4 = jnp.concatenate([cos] * NPAIR, axis=1)
    sin4 = jnp.concatenate([sin] * NPAIR, axis=1)
    q_ref[...] = ((qn * cos4 + _swap16(qn) * sin4) * (HD ** -0.5)).astype(BF16)

    zk = _dot(hb, win_ref[:, 4 * HALF:4 * HALF + 256])
    kn = zk * lax.rsqrt(_head_sum(zk * zk, e_ref[0:256, 0:256]) * (1.0 / HD) + EPS) * kg_ref[...]
    lane = lax.broadcasted_iota(jnp.int32, (TS, 128), 1)
    ck_ref[...] = jnp.where(lane < HD, kn[:, 0:128], kn[:, 128:256])
    cos2 = jnp.concatenate([cos, cos], axis=1)
    sin2 = jnp.concatenate([sin, sin], axis=1)
    kd_ref[...] = (kn * cos2 + _swap16(kn) * sin2).astype(BF16)
    zv = _dot(hb, win_ref[:, 4 * HALF + 256:4 * HALF + 512])
    cv_ref[...] = jnp.where(lane < HD, zv[:, 0:128], zv[:, 128:256])
    vd_ref[...] = zv.astype(BF16)


def _pre_cd(x, mod, g, win, lw1, lw2, la1, la2, lg1, lg2, mu_rkv, mu_wag, w0, a0, k_k, qg, kg, e, cos, sin):
    mi = _mod_index(NS_CTX)
    nblk8 = ROWS // 8
    row512 = lambda i: (i, 0)
    f32_out = jax.ShapeDtypeStruct((ROWS, HALF), F32)
    out_shape = [f32_out] * 9 + [jax.ShapeDtypeStruct((ROWS, HALF), BF16),
                                 jax.ShapeDtypeStruct((ROWS, 256), BF16),
                                 jax.ShapeDtypeStruct((ROWS, 256), BF16),
                                 jax.ShapeDtypeStruct((ROWS, 128), F32),
                                 jax.ShapeDtypeStruct((ROWS, 128), F32)]
    out_specs = [pl.BlockSpec((TS, HALF), row512)] * 10 + [pl.BlockSpec((TS, 256), row512)] * 2 + \
                [pl.BlockSpec((TS, 128), row512)] * 2
    return pl.pallas_call(
        _pre_cd_body,
        out_shape=out_shape,
        grid=(NS,),
        in_specs=[pl.BlockSpec((TS, D), row512),
                  pl.BlockSpec((8, D), lambda i: (jnp.maximum(i * (TS // 8) - 1, 0), 0)),
                  pl.BlockSpec((8, D), lambda i: (jnp.minimum((i + 1) * (TS // 8), nblk8 - 1), 0)),
                  pl.BlockSpec((1, 6, D), lambda i: (mi(i, TS), 0, 0)),
                  _const_spec((1, D)), _const_spec(win.shape),
                  _const_spec(lw1.shape), _const_spec(lw2.shape), _const_spec(la1.shape), _const_spec(la2.shape),
                  _const_spec(lg1.shape), _const_spec(lg2.shape),
                  _const_spec((3, HALF)), _const_spec((3, D)), _const_spec((2, HALF)), _const_spec((2, HALF)),
                  _const_spec((1, HALF)), _const_spec((1, HALF)), _const_spec((1, 256)),
                  _const_spec((HALF, HALF)),
                  pl.BlockSpec((TS, 128), lambda i: (i % (L_LAT // TS), 0)),
                  pl.BlockSpec((TS, 128), lambda i: (i % (L_LAT // TS), 0))],
        out_specs=out_specs,
        compiler_params=_params(("arbitrary",)),
        name="pre_cd",
    )(x, x, x, mod, g, win, lw1, lw2, la1, la2, lg1, lg2, mu_rkv, mu_wag, w0, a0, k_k, qg, kg, e, cos, sin)


def _scan_unit(rc, kc, vc, kkc, lwc, ac, ka, s_prev, reverse, masks):
    cum_mask, tri_s, tri_i, eye, lane_lo, diag8, offs = masks
    lw_hi = lwc.astype(BF16)
    res = lwc - lw_hi.astype(F32)
    lw_mid = res.astype(BF16)
    lw_lo = (res - lw_mid.astype(F32)).astype(BF16)
    cum3 = _dot(cum_mask, jnp.concatenate([lw_hi, lw_mid, lw_lo], axis=1))
    cum = cum3[:, 0:128] + cum3[:, 128:256] + cum3[:, 256:384]
    cume = cum - lwc
    if reverse:
        tot, mid = cum[0:1, :], cum[SC // 2:SC // 2 + 1, :]
    else:
        tot, mid = cum[SC - 1:SC, :], cum[SC // 2 - 1:SC // 2, :]
    at = -kkc * jnp.exp(cume - mid)
    rt = rc * jnp.exp(cum - mid)
    e3 = jnp.exp(mid - cum)
    kt = kc * (1.0 + (ac - 1.0) * ka) * e3
    bt = kkc * ac * e3

    def stack(x):
        return jnp.concatenate([jnp.where(lane_lo, x, 0.0), jnp.where(lane_lo, 0.0, x)], axis=0)

    a2 = stack(at)
    r2 = stack(rt)
    v2 = stack(vc).astype(BF16)
    kb = jnp.concatenate([stack(kt), stack(bt)], axis=0).astype(BF16)
    lhs = jnp.concatenate([a2, r2], axis=0).astype(BF16)
    rhs = jnp.concatenate([bt, bt, kt, kt], axis=0).astype(BF16)
    yield
    p = _dot_nt(lhs, rhs)
    a_ab = jnp.where(tri_s, p[0:128, 0:128], 0.0)
    a_ak = jnp.where(tri_s, p[0:128, 128:256], 0.0)
    a_rb = jnp.where(tri_i, p[128:256, 0:128], 0.0)
    a_rk = jnp.where(tri_i, p[128:256, 128:256], 0.0)

    lk = jnp.where(diag8, a_ab, 0.0)
    tinv = eye + lk
    for _ in range(2):
        lkb = lk.astype(BF16)
        yield
        lk = _dot(lkb, lkb)
        yield
        tinv = _dot(tinv.astype(BF16), (eye + lk).astype(BF16))
    for off in offs:
        tb = tinv.astype(BF16)
        yield
        tl = _dot(tb, jnp.where(off, a_ab, 0.0).astype(BF16)).astype(BF16)
        yield
        tinv = tinv + _dot(tl, tb)

    akv = _dot(a_ak.astype(BF16), v2)
    yield
    wu = _dot(tinv.astype(BF16), jnp.concatenate([a2, akv], axis=1).astype(BF16))
    q = jnp.concatenate([jnp.concatenate([jnp.zeros((128, 128), BF16), v2], axis=1),
                         wu.astype(BF16)], axis=0)
    yield
    ry = _dot(jnp.concatenate([a_rk, a_rb], axis=1).astype(BF16), q)
    rw2 = r2 + ry[:, 0:128]
    y02 = ry[:, 128:256]
    gh = _dot_tn(q, kb)

    sd = s_prev * jnp.exp(mid)
    sdb = sd.astype(BF16)
    yield
    ys = _dot_nt(rw2.astype(BF16), sdb)
    sg = _dot(sdb, gh[0:128, :].astype(BF16))
    s_new = s_prev * jnp.exp(tot) + (sg + gh[128:256, :]) * jnp.exp(tot - mid)
    y2 = y02 + ys
    return y2[0:SC, :] + y2[SC:2 * SC, :], s_new


def _run_lockstep(gens):
    results = [None] * len(gens)
    live = list(range(len(gens)))
    while live:
        still = []
        for u in live:
            try:
                next(gens[u])
                still.append(u)
            except StopIteration as stop:
                results[u] = stop.value
        live = still
    return results


def _scan_body(tab_ref, *refs):
    n_in = NMEM * 2 * 6
    in_refs = refs[:n_in]
    ka_ref = refs[n_in]
    s0_refs = refs[n_in + 1:n_in + 1 + 2 * NMEM]
    yf_ref, yb_ref, sf_ref, sb_ref, st_ref = refs[n_in + 1 + 2 * NMEM:]
    t = pl.program_id(0)
    first = tab_ref[_T_FIRST, t]
    last = tab_ref[_T_LAST, t]
    is_lat = tab_ref[_T_LAT, t]

    def unit(m, d, p):
        return (m * 2 + d) * NPAIR + p

    @pl.when(first == 1)
    def _():
        zero = jnp.zeros((HD, HD), F32)
        lat_f = is_lat.astype(F32)
        for m in range(NMEM):
            for d in range(2):
                for p in range(NPAIR):
                    lo = s0_refs[m * 2 + d][0, 2 * p] * lat_f
                    hi = s0_refs[m * 2 + d][0, 2 * p + 1] * lat_f
                    st_ref[unit(m, d, p)] = jnp.concatenate(
                        [jnp.concatenate([lo, zero], axis=1), jnp.concatenate([zero, hi], axis=1)], axis=0)

    ri = lax.broadcasted_iota(jnp.int32, (SC, SC), 0)
    ci = lax.broadcasted_iota(jnp.int32, (SC, SC), 1)
    r2 = lax.broadcasted_iota(jnp.int32, (128, 128), 0)
    c2 = lax.broadcasted_iota(jnp.int32, (128, 128), 1)
    same = (r2 & SC) == (c2 & SC)
    eye = (r2 == c2).astype(F32)
    diag8 = (r2 >> 3) == (c2 >> 3)
    lane_lo = lax.broadcasted_iota(jnp.int32, (SC, 128), 1) < HD

    gens, dests = [], []
    for d, reverse in enumerate((False, True)):
        early, late = (r2, c2) if reverse else (c2, r2)
        offs = tuple(((r2 >> sh + 1) == (c2 >> sh + 1)) & ((late & (1 << sh)) != 0) & ((early & (1 << sh)) == 0)
                     for sh in (3, 4, 5))
        if reverse:
            masks = ((ri <= ci).astype(BF16), same & (r2 < c2), same & (r2 <= c2), eye, lane_lo, diag8, offs)
        else:
            masks = ((ri >= ci).astype(BF16), same & (r2 > c2), same & (r2 >= c2), eye, lane_lo, diag8, offs)
        y_ref = yb_ref if reverse else yf_ref
        for m in range(NMEM):
            unit_refs = in_refs[(m * 2 + d) * 6:(m * 2 + d + 1) * 6]
            for p in range(NPAIR):
                ls = slice(p * 128, (p + 1) * 128)
                rc, kc, vc, kkc, lwc, ac = (ref[:, ls] for ref in unit_refs)
                gens.append(_scan_unit(rc, kc, vc, kkc, lwc, ac, ka_ref[:, ls], st_ref[unit(m, d, p)],
                                       reverse, masks))
                dests.append((y_ref, m, ls, unit(m, d, p)))
    for (y_ref, m, ls, u), (y, s_new) in zip(dests, _run_lockstep(gens)):
        y_ref[m, 0, :, ls] = y
        st_ref[u] = s_new

    @pl.when(last == 1)
    def _():
        for m in range(NMEM):
            for d, s_out in enumerate((sf_ref, sb_ref)):
                for p in range(NPAIR):
                    s = st_ref[unit(m, d, p)]
                    s_out[m, 0, 2 * p] = s[0:HD, 0:HD]
                    s_out[m, 0, 2 * p + 1] = s[HD:2 * HD, HD:2 * HD]


_T_FWD, _T_BWD = 0, NMEM
_T_YF, _T_YB = 2 * NMEM, 2 * NMEM + 1
_T_FIRST, _T_LAST, _T_LAT = 2 * NMEM + 2, 2 * NMEM + 3, 2 * NMEM + 4
_T_SIN = 2 * NMEM + 5
_T_SOUT = 3 * NMEM + 5
_Y_SLOTS_CTX = ROWS_CTX // SC // NMEM


def _scan_tables():
    cols = []
    for group, (nb, ln, base, slot0) in enumerate(((N_CTX, L_CTX, 0, 0), (N_LAT, L_LAT, ROWS_CTX, _Y_SLOTS_CTX))):
        nc = ln // SC
        for p in range(nb // NMEM):
            batches = [p + m * (nb // NMEM) for m in range(NMEM)]
            blks = [(base + b * ln) // SC for b in batches]
            for c in range(nc):
                col = [blk + c for blk in blks] + [blk + nc - 1 - c for blk in blks]
                col += [slot0 + p * nc + c, slot0 + p * nc + nc - 1 - c]
                col += [int(c == 0), int(c == nc - 1), group]
                col += [b if group == 1 else 0 for b in batches]
                col += [p if group == 0 else N_CTX // NMEM]
                cols.append(col)
    return np.asarray(cols, np.int32).T


def _scan(r, k, v, kk, lwf, lwb, af, ab, k_a, s0f, s0b):
    tab = _scan_tables()
    steps = tab.shape[1]

    def row_spec(table_row):
        return pl.BlockSpec((SC, HALF), lambda t, tab: (tab[table_row, t], 0))

    def state_spec(table_row):
        return pl.BlockSpec((1, 2 * NPAIR, HD, HD), lambda t, tab: (tab[table_row, t], 0, 0, 0))

    in_specs, args = [], []
    for m in range(NMEM):
        for table_row, lw, a in ((_T_FWD + m, lwf, af), (_T_BWD + m, lwb, ab)):
            in_specs += [row_spec(table_row)] * 6
            args += [r, k, v, kk, lw, a]
    in_specs.append(pl.BlockSpec((1, HALF), lambda t, tab: (0, 0)))
    args.append(k_a)
    for m in range(NMEM):
        in_specs += [state_spec(_T_SIN + m)] * 2
        args += [s0f, s0b]
    y_shape = jax.ShapeDtypeStruct((NMEM, ROWS // SC // NMEM, SC, HALF), F32)
    s_shape = jax.ShapeDtypeStruct((NMEM, N_CTX // NMEM + 1, 2 * NPAIR, HD, HD), F32)
    s_out = pl.BlockSpec((NMEM, 1, 2 * NPAIR, HD, HD), lambda t, tab: (0, tab[_T_SOUT, t], 0, 0, 0))
    grid_spec = pltpu.PrefetchScalarGridSpec(
        num_scalar_prefetch=1,
        grid=(steps,),
        in_specs=in_specs,
        out_specs=[pl.BlockSpec((NMEM, 1, SC, HALF), lambda t, tab: (0, tab[_T_YF, t], 0, 0)),
                   pl.BlockSpec((NMEM, 1, SC, HALF), lambda t, tab: (0, tab[_T_YB, t], 0, 0)),
                   s_out, s_out],
        scratch_shapes=[pltpu.VMEM((NMEM * 2 * NPAIR, 128, 128), F32)],
    )
    return pl.pallas_call(
        _scan_body,
        out_shape=[y_shape, y_shape, s_shape, s_shape],
        grid_spec=grid_spec,
        compiler_params=_params(("arbitrary",)),
        name="wkv_scan",
    )(jnp.asarray(tab), *args)


def _attn_body(q_ref, k_ref, v_ref, *rest, cached):
    if cached:
        ck_ref, cv_ref, o_ref = rest
    else:
        (o_ref,) = rest
    lane_lo = lax.broadcasted_iota(jnp.int32, (QB, 128), 1) < HD

    def head(pr, half):
        qp = q_ref[:, pr * 128:(pr + 1) * 128]
        kvh = (2 * pr) // 4
        ks = slice(kvh * 128, (kvh + 1) * 128)
        qh = jnp.where(lane_lo if half == 0 else jnp.logical_not(lane_lo), qp, jnp.zeros_like(qp))
        s_new = _dot_nt(qh, k_ref[:, ks])
        if cached:
            s_old = _dot_nt(qh, ck_ref[0, :, ks])
            yield
            m = jnp.maximum(jnp.max(s_new, -1, keepdims=True), jnp.max(s_old, -1, keepdims=True))
            p_new = jnp.exp(s_new - m)
            p_old = jnp.exp(s_old - m)
            den = jnp.sum(p_new, -1, keepdims=True) + jnp.sum(p_old, -1, keepdims=True)
            yield
            o = _dot(p_new.astype(BF16), v_ref[:, ks]) + _dot(p_old.astype(BF16), cv_ref[0, :, ks])
        else:
            yield
            m = jnp.max(s_new, -1, keepdims=True)
            p_new = jnp.exp(s_new - m)
            den = jnp.sum(p_new, -1, keepdims=True)
            yield
            o = _dot(p_new.astype(BF16), v_ref[:, ks])
        return o / den

    outs = _run_lockstep([head(pr, half) for pr in range(NPAIR) for half in range(2)])
    for pr in range(NPAIR):
        o_ref[:, pr * 128:(pr + 1) * 128] = jnp.where(lane_lo, outs[2 * pr], outs[2 * pr + 1]).astype(BF16)


def _attention(q, kd, vd, base, nb, ln, cache=None):
    nq = ln // QB
    blk0 = base // QB
    seq0 = base // ln
    in_specs = [pl.BlockSpec((QB, HALF), lambda b, j: (blk0 + b * nq + j, 0)),
                pl.BlockSpec((ln, 256), lambda b, j: (seq0 + b, 0)),
                pl.BlockSpec((ln, 256), lambda b, j: (seq0 + b, 0))]
    args = [q, kd, vd]
    if cache is not None:
        in_specs += [pl.BlockSpec((1, PAST, 256), lambda b, j: (b, 0, 0))] * 2
        args += list(cache)
    return pl.pallas_call(
        functools.partial(_attn_body, cached=cache is not None),
        out_shape=jax.ShapeDtypeStruct((nb * ln, HALF), BF16),
        grid=(nb, nq),
        in_specs=in_specs,
        out_specs=pl.BlockSpec((QB, HALF), lambda b, j: (b * nq + j, 0)),
        compiler_params=_params(("arbitrary", "arbitrary")),
        name="gqa_cached" if cache is not None else "gqa",
    )(*args)


def _post_cd_body(x_ref, mod_ref, yf_ref, yb_ref, r_ref, k_ref, v_ref, af_ref, ab_ref, gg_ref,
                  atc_ref, atl_ref, wout_ref, e_ref, lng_ref, lnb_ref, rk_ref, ka_ref, o_ref):
    i = pl.program_id(0)
    gate = mod_ref[0, 2:3, :]
    y = yf_ref[0].reshape(TS, HALF) + yb_ref[0].reshape(TS, HALF)
    yc = y - _head_sum(y, e_ref) * (1.0 / HD)
    yn = yc * lax.rsqrt(_head_sum(yc * yc, e_ref) * (1.0 / HD) + GN_EPS) * lng_ref[...] + lnb_ref[...]
    kd2 = k_ref[...] * (2.0 + (af_ref[...] + ab_ref[...] - 2.0) * ka_ref[...])
    bonus = _head_sum(r_ref[...] * kd2 * rk_ref[...], e_ref) * v_ref[...]
    yc_out = ((yn + bonus) * gg_ref[...]).astype(BF16)
    att = jnp.where(i >= NS_CTX, atl_ref[...], atc_ref[...])
    out = _dot(yc_out, wout_ref[0:HALF, :]) + _dot(att, wout_ref[HALF:2 * HALF, :])
    o_ref[...] = x_ref[...] + gate * out


def _post_cd(x, mod, yf, yb, r, k, v, af, ab, gg, att_c, att_l, wout, e, lng, lnb, r_k, k_a):
    mi = _mod_index(NS_CTX)
    row = lambda i: (i, 0)
    half = pl.BlockSpec((TS, HALF), row)
    ctx_per_mem = NS_CTX // NMEM
    lat_per_mem = (NS - NS_CTX) // NMEM

    def y_index(i):
        j = i - NS_CTX
        return (jnp.where(i < NS_CTX, i // ctx_per_mem, j // lat_per_mem),
                jnp.where(i < NS_CTX, i % ctx_per_mem, ctx_per_mem + j % lat_per_mem), 0, 0)

    y_spec = pl.BlockSpec((1, TS // SC, SC, HALF), y_index)
    return pl.pallas_call(
        _post_cd_body,
        out_shape=jax.ShapeDtypeStruct((ROWS, D), F32),
        grid=(NS,),
        in_specs=[pl.BlockSpec((TS, D), row),
                  pl.BlockSpec((1, 6, D), lambda i: (mi(i, TS), 0, 0)), y_spec, y_spec] + [half] * 6 +
                 [pl.BlockSpec((TS, HALF), lambda i: (jnp.minimum(i, NS_CTX - 1), 0)),
                  pl.BlockSpec((TS, HALF), lambda i: (jnp.maximum(i - NS_CTX, 0), 0)),
                  _const_spec((D, D)), _const_spec((HALF, HALF)),
                  _const_spec((1, HALF)), _const_spec((1, HALF)), _const_spec((1, HALF)), _const_spec((1, HALF))],
        out_specs=pl.BlockSpec((TS, D), row),
        compiler_params=_params(("arbitrary",)),
        name="post_cd",
    )(x, mod, yf, yb, r, k, v, af, ab, gg, att_c, att_l, wout, e, lng, lnb, r_k, k_a)


def _rope_tables():
    half = 16
    inv = 10000.0 ** (-jnp.arange(half, dtype=F32) / half)
    t = jnp.arange(L_LAT)
    cos_parts, sin_parts = [], []
    for pos in (t // GRID_W, t % GRID_W):
        ang = pos.astype(F32)[:, None] * inv[None, :]
        cos_parts += [jnp.cos(ang), jnp.cos(ang)]
        sin_parts += [-jnp.sin(ang), jnp.sin(ang)]
    cos = jnp.concatenate(cos_parts * 2, axis=1)
    sin = jnp.concatenate(sin_parts * 2, axis=1)
    return cos, sin


def kernel(x_prompt, x_sample, cache_k, cache_v, state_wkv_fwd, state_wkv_bwd, c, c_ctx, mod_w, mod_b, norm1_g, norm2_g, w_out, ffn_up, ffn_conv_w, ffn_conv_b, ffn_down, final_g, ab_w_in, a_vnorm_g, a_vnorm_b, a_ws, a_bs, b_conv_w, b_conv_b, b_norm_g, cd_w_in, c_mu_rkv, c_mu_wag, c_w0, c_w1, c_w2, c_a0, c_a1, c_a2, c_g1, c_g2, c_k_k, c_k_a, c_r_k, c_ln_g, c_ln_b, d_q_g, d_k_g):
    row = lambda a: a.reshape(1, -1)
    cvec =jnp.concatenate([c_ctx[None, :], c, jnp.zeros((16 - 1 - N_LAT, D), F32)], axis=0)
    mods = _modulation(cvec, mod_w, mod_b).reshape(2, 16, 6, D)

    bs = jnp.broadcast_to(a_bs[0][:, :, None], (A_GROUPS, CHUNK, 128))
    cw = jnp.concatenate([b_conv_w[0], jnp.zeros((1, HALF), F32)], axis=0)
    x = _mix_ab(x_prompt.reshape(ROWS_CTX, D), x_sample.reshape(ROWS_LAT, D), mods[0], row(norm1_g[0]),
                ab_w_in[0].astype(BF16), w_out[0].astype(BF16),
                a_ws[0].astype(BF16), bs, row(a_vnorm_g[0]), row(a_vnorm_b[0]),
                cw, row(b_conv_b[0]), row(b_norm_g[0]))
    x = _ffn(x, 0, NT, mods[0], row(norm2_g[0]), ffn_up[0].astype(BF16), ffn_conv_w[0], row(ffn_conv_b[0]),
             ffn_down[0].astype(BF16), row(final_g), final=False)

    w = cd_w_in[0]
    wk = w[:, 4 * HALF:4 * HALF + 128]
    wv = w[:, 4 * HALF + 128:4 * HALF + 256]
    win = jnp.concatenate([w[:, :4 * HALF], wk[:, :HD], wk[:, :HD], wk[:, HD:], wk[:, HD:],
                           wv[:, :HD], wv[:, :HD], wv[:, HD:], wv[:, HD:]], axis=1).astype(BF16)
    lw1 = jnp.concatenate([c_w1[0, 0], c_w1[0, 1]], axis=1).astype(BF16)
    la1 = jnp.concatenate([c_a1[0, 0], c_a1[0, 1]], axis=1).astype(BF16)
    ones_blk = jnp.asarray(np.kron(np.eye(HALF // HD), np.ones((HD, HD))), BF16)
    cos, sin = _rope_tables()
    k_a = row(c_k_a[0])
    (r, k, v, kk, lwf, lwb, af, ab, gg, q, kd, vd, ck, cv) = _pre_cd(
        x, mods[1], row(norm1_g[1]), win, lw1, c_w2[0].astype(BF16), la1, c_a2[0].astype(BF16),
        c_g1[0].astype(BF16), c_g2[0].astype(BF16), c_mu_rkv[0], c_mu_wag[0], c_w0[0], c_a0[0],
        row(c_k_k[0]), jnp.tile(d_q_g[0], HALF // HD)[None, :], jnp.tile(d_k_g[0], 4)[None, :],
        ones_blk, cos, sin)

    yf, yb, s_f, s_b = _scan(r, k, v, kk, lwf, lwb, af, ab, k_a, state_wkv_fwd[:, 0], state_wkv_bwd[:, 0])

    def dup_cache(t):
        t = t[:, 0].astype(BF16)
        return jnp.concatenate([t[:, :, 0], t[:, :, 0], t[:, :, 1], t[:, :, 1]], axis=-1)

    att_c = _attention(q, kd, vd, 0, N_CTX, L_CTX)
    att_l = _attention(q, kd, vd, ROWS_CTX, N_LAT, L_LAT, cache=(dup_cache(cache_k), dup_cache(cache_v)))

    x = _post_cd(x, mods[1], yf, yb, r, k, v, af, ab, gg, att_c, att_l, w_out[1].astype(BF16), ones_blk,
                 row(c_ln_g[0]), row(c_ln_b[0]), row(c_r_k[0]), k_a)
    ffn1 = (mods[1], row(norm2_g[1]), ffn_up[1].astype(BF16), ffn_conv_w[1], row(ffn_conv_b[1]),
            ffn_down[1].astype(BF16), row(final_g))
    y_prompt = _ffn(x, 0, NT_CTX, *ffn1, final=True).reshape(N_CTX, L_CTX, D)
    y_sample = _ffn(x, NT_CTX, NT - NT_CTX, *ffn1, final=True).reshape(N_LAT, L_LAT, D)
    new_k = ck[:ROWS_CTX].reshape(N_CTX, 1, L_CTX, 2, HD)
    new_v = cv[:ROWS_CTX].reshape(N_CTX, 1, L_CTX, 2, HD)
    return (y_prompt, y_sample, new_k, new_v,
            s_f[:, :N_CTX // NMEM].reshape(N_CTX, 1, 2 * NPAIR, HD, HD),
            s_b[:, :N_CTX // NMEM].reshape(N_CTX, 1, 2 * NPAIR, HD, HD))
```

```python
import functools
import math

import jax
import jax.numpy as jnp
import numpy as np
from jax import lax
from jax.experimental import pallas as pl
from jax.experimental.pallas import tpu as pltpu

F32 = jnp.float32
BF16 = jnp.bfloat16

D = 1024
N_CTX, L_CTX = 16, 256
N_LAT, L_LAT = 8, 1024
ROWS_CTX = N_CTX * L_CTX
ROWS_LAT = N_LAT * L_LAT
ROWS = ROWS_CTX + ROWS_LAT
PAST = 256
GRID_W = 64
EPS = 1e-6
GN_EPS = 64e-5
HALF = 512
HD = 64
NPAIR = HALF // 128
D_FF = 2816
B_CONV = 31
CHUNK = 128
A_GROUPS = 4

TM = 1024
NT = ROWS // TM
NT_CTX = ROWS_CTX // TM
SEG = 256
NSEG = TM // SEG
HALO = 16
TS = 512
NS = ROWS // TS
NS_CTX = ROWS_CTX // TS
FB = 256
NJ = D_FF // FB
SC = 64
NMEM = 2
QB = 256
VMEM_LIMIT = 56 * 1024 * 1024

_NT_DIMS = (((1,), (1,)), ((), ()))
_TN_DIMS = (((0,), (0,)), ((), ()))


def _dot(a, b):
    return jnp.dot(a, b, preferred_element_type=F32)


def _dot_nt(a, b):
    return lax.dot_general(a, b, _NT_DIMS, preferred_element_type=F32)


def _dot_tn(a, b):
    return lax.dot_general(a, b, _TN_DIMS, preferred_element_type=F32)


def _rms(x, eps=EPS):
    return x * lax.rsqrt(jnp.mean(x * x, -1, keepdims=True) + eps)


def _sigmoid(x):
    return 1.0 / (1.0 + jnp.exp(-x))


def _silu(x):
    return x * _sigmoid(x)


def _gelu_tanh(x):
    return 0.5 * x * (1.0 + jnp.tanh(math.sqrt(2.0 / math.pi) * (x + 0.044715 * (x * x * x))))


def _const_spec(shape):
    return pl.BlockSpec(shape, lambda *_: (0,) * len(shape), pipeline_mode=pl.Buffered(1))


def _layer_spec(shape, layer):
    return pl.BlockSpec((None,) + tuple(shape), lambda *_: (layer,) + (0,) * len(shape), pipeline_mode=pl.Buffered(1))


def _params(sem):
    return pltpu.CompilerParams(dimension_semantics=sem, vmem_limit_bytes=VMEM_LIMIT)


def _mod_body(c_ref, w_ref, b_ref, o_ref):
    s = _silu(c_ref[...])
    o_ref[0] = jnp.dot(s, w_ref[0], preferred_element_type=F32,
                       precision=lax.Precision.HIGHEST) + b_ref[0]


def _modulation(cvec, mod_w, mod_b):
    depth = mod_w.shape[0]
    nb = 1536
    return pl.pallas_call(
        _mod_body,
        out_shape=jax.ShapeDtypeStruct((depth, 16, 6 * D), F32),
        grid=(depth, 6 * D // nb),
        in_specs=[pl.BlockSpec((16, D), lambda l, j: (0, 0)),
                  pl.BlockSpec((1, D, nb), lambda l, j: (l, 0, j)),
                  pl.BlockSpec((1, 1, nb), lambda l, j: (l, 0, j))],
        out_specs=pl.BlockSpec((1, 16, nb), lambda l, j: (l, 0, j)),
        compiler_params=_params(("arbitrary", "arbitrary")),
        name="modulation",
    )(cvec, mod_w, mod_b.reshape(depth, 1, 6 * D))


def _mod_index(tiles_ctx):
    def idx(i, tile_rows):
        return jnp.where(i >= tiles_ctx, (i - tiles_ctx) // (L_LAT // tile_rows) + 1, 0)
    return idx


def _mix_ab_body(xc_ref, xl_ref, mod_ref, g_ref, win_ref, wout_ref, ws_ref, bs_ref, vg_ref, vb_ref,
                 cw_ref, cb_ref, ng_ref, o_ref, cat_ref, pad_ref, shift_ref):
    i = pl.program_id(0)
    is_lat = i >= NT_CTX
    lat = is_lat.astype(F32)

    def x_rows(r0):
        return jnp.where(is_lat, xl_ref[pl.ds(r0, SEG), :], xc_ref[pl.ds(r0, SEG), :])

    shift = mod_ref[0, 0:1, :]
    scale = mod_ref[0, 1:2, :]
    gate = mod_ref[0, 2:3, :]

    def phase1(s, carry):
        r0 = pl.multiple_of(s * SEG, SEG)
        x = x_rows(r0)
        h = (_rms(x) * g_ref[...] * (1.0 + scale) + shift).astype(BF16)
        z = _dot(h, win_ref[...])
        za = _gelu_tanh(z[:, :2 * HALF])
        u = za[:, :HALF]
        v = za[:, HALF:]
        vc = v - jnp.mean(v, -1, keepdims=True)
        vn = vc * lax.rsqrt(jnp.mean(vc * vc, -1, keepdims=True) + EPS) * vg_ref[...] + vb_ref[...]
        vnb = vn.astype(BF16)
        for n in range(SEG // CHUNK):
            parts = []
            for g in range(A_GROUPS):
                vv = vnb[n * CHUNK:(n + 1) * CHUNK, g * 128:(g + 1) * 128]
                parts.append(_dot(ws_ref[g], vv) + bs_ref[g])
            vp = jnp.concatenate(parts, axis=1)
            ya = u[n * CHUNK:(n + 1) * CHUNK, :] * vp
            cat_ref[pl.ds(pl.multiple_of(r0 + n * CHUNK, CHUNK), CHUNK), 0:HALF] = ya.astype(BF16)
        zb = z[:, 2 * HALF:]
        pad_ref[s, HALO:HALO + SEG, :] = zb[:, :HALF] * _sigmoid(zb[:, HALF:])
        return carry

    lax.fori_loop(0, NSEG, phase1, 0)

    zeros_h = jnp.zeros((HALO, HALF), F32)
    for s in range(NSEG):
        left = pad_ref[s - 1, SEG:SEG + HALO, :] * lat if s > 0 else zeros_h
        right = pad_ref[s + 1, HALO:2 * HALO, :] * lat if s < NSEG - 1 else zeros_h
        pad_ref[s, 0:HALO, :] = left
        pad_ref[s, HALO + SEG:2 * HALO + SEG, :] = right

    RB = 32

    shift_rows = SEG + 2 * HALO - 8

    def phase3(s, carry):
        r0 = pl.multiple_of(s * SEG, SEG)
        for sh in range(1, 8):
            shift_ref[sh - 1] = pad_ref[s, sh:sh + shift_rows, :]
        for rb in range(SEG // RB):
            acc = jnp.zeros((RB, HALF), F32) + cb_ref[...]
            for k in range(B_CONV):
                q8, sh = divmod(HALO - B_CONV // 2 + k, 8)
                off = 8 * q8 + rb * RB
                win = pad_ref[s, off:off + RB, :] if sh == 0 else shift_ref[sh - 1, off:off + RB, :]
                acc = acc + cw_ref[k:k + 1, :] * win
            yb = _silu(_rms(acc) * ng_ref[...])
            cat_ref[pl.ds(pl.multiple_of(r0 + rb * RB, RB), RB), HALF:2 * HALF] = yb.astype(BF16)
        out = _dot(cat_ref[pl.ds(r0, SEG), :], wout_ref[...])
        o_ref[pl.ds(r0, SEG), :] = x_rows(r0) + gate * out
        return carry

    lax.fori_loop(0, NSEG, phase3, 0)


def _mix_ab(xc, xl, mod, g, win, wout, ws, bs, vg, vb, cw, cb, ng):
    mi = _mod_index(NT_CTX)
    return pl.pallas_call(
        _mix_ab_body,
        out_shape=jax.ShapeDtypeStruct((ROWS, D), F32),
        grid=(NT,),
        in_specs=[pl.BlockSpec((TM, D), lambda i: (jnp.minimum(i, NT_CTX - 1), 0)),
                  pl.BlockSpec((TM, D), lambda i: (jnp.maximum(i - NT_CTX, 0), 0)),
                  pl.BlockSpec((1, 6, D), lambda i: (mi(i, TM), 0, 0)),
                  _const_spec((1, D)), _const_spec((D, 4 * HALF)), _layer_spec((D, D), 0),
                  _const_spec((A_GROUPS, CHUNK, CHUNK)), _const_spec((A_GROUPS, CHUNK, 128)),
                  _const_spec((1, HALF)), _const_spec((1, HALF)),
                  _const_spec((B_CONV + 1, HALF)), _const_spec((1, HALF)), _const_spec((1, HALF))],
        out_specs=pl.BlockSpec((TM, D), lambda i: (i, 0)),
        scratch_shapes=[pltpu.VMEM((TM, D), BF16),
                        pltpu.VMEM((NSEG, SEG + 2 * HALO, HALF), F32),
                        pltpu.VMEM((7, SEG + 2 * HALO - 8, HALF), F32)],
        compiler_params=_params(("arbitrary",)),
        name="mix_ab",
    )(xc, xl, mod, g, win, wout, ws, bs, vg, vb, cw, cb, ng)


def _ffn_body(x_ref, mod_ref, g_ref, up_ref, cw_ref, cb_ref, down_ref, fg_ref, o_ref,
              h_ref, gated_ref, cza_ref, czb_ref, *, tile0, final):
    is_ctx = tile0 + pl.program_id(0) < NT_CTX
    shift = mod_ref[0, 3:4, :]
    scale = mod_ref[0, 4:5, :]
    h_ref[...] = (_rms(x_ref[...]) * g_ref[...] * (1.0 + scale) + shift).astype(BF16)

    def up_proj(col):
        return _dot(h_ref[...], up_ref[:, col * FB:(col + 1) * FB])

    def conv_branch(col, z, cz_ref):
        cs = slice(col * FB, (col + 1) * FB)
        w0, w1, w2, b = cw_ref[0:1, cs], cw_ref[1:2, cs], cw_ref[2:3, cs], cb_ref[:, cs]
        cz_ref[...] = w0 * pltpu.roll(z, 1, 0) + w1 * z + w2 * pltpu.roll(z, TM - 1, 0) + b
        for t0 in range(0, TM, L_CTX):
            t1 = t0 + L_CTX - 1
            first = w1 * z[t0:t0 + 1, :] + w2 * z[t0 + 1:t0 + 2, :] + b
            last = w0 * z[t1 - 1:t1, :] + w1 * z[t1:t1 + 1, :] + b
            cz_ref[t0:t0 + 1, :] = first if t0 == 0 else jnp.where(is_ctx, first, cz_ref[t0:t0 + 1, :])
            cz_ref[t1:t1 + 1, :] = last if t1 == TM - 1 else jnp.where(is_ctx, last, cz_ref[t1:t1 + 1, :])

    for j in range(NJ):
        conv_branch(j, up_proj(j), cza_ref)
        conv_branch(NJ + j, up_proj(NJ + j), czb_ref)
        gated_ref[:, j * FB:(j + 1) * FB] = (_silu(czb_ref[...]) * cza_ref[...]).astype(BF16)

    y = x_ref[...] + mod_ref[0, 5:6, :] * _dot(gated_ref[...], down_ref[...])
    if final:
        y = _rms(y) * fg_ref[...]
    o_ref[...] = y


def _ffn(x, tile0, ntiles, layer, mod, g, up, cw, cb, down, fg, final):
    mi = _mod_index(NT_CTX)
    return pl.pallas_call(
        functools.partial(_ffn_body, tile0=tile0, final=final),
        out_shape=jax.ShapeDtypeStruct((ntiles * TM, D), F32),
        grid=(ntiles,),
        in_specs=[pl.BlockSpec((TM, D), lambda i: (tile0 + i, 0)),
                  pl.BlockSpec((1, 6, D), lambda i: (mi(tile0 + i, TM), 0, 0)),
                  _const_spec((1, D)),
                  _layer_spec((D, 2 * D_FF), layer), _layer_spec((3, 2 * D_FF), layer), _const_spec((1, 2 * D_FF)),
                  _layer_spec((D_FF, D), layer), _const_spec((1, D))],
        out_specs=pl.BlockSpec((TM, D), lambda i: (i, 0)),
        scratch_shapes=[pltpu.VMEM((TM, D), BF16), pltpu.VMEM((TM, D_FF), BF16),
                        pltpu.VMEM((TM, FB), F32), pltpu.VMEM((TM, FB), F32)],
        compiler_params=_params(("arbitrary",)),
        name="conv_ffn",
    )(x, mod, g, up, cw, cb, down, fg)


def _head_sum(x, e_ref):
    return _dot(x.astype(BF16), e_ref[...])


def _swap16(x):
    lane = lax.broadcasted_iota(jnp.int32, x.shape, 1)
    n = x.shape[1]
    return jnp.where((lane & 16) == 0, pltpu.roll(x, n - 16, 1), pltpu.roll(x, 16, 1))


def _pre_cd_body(x_ref, xp_ref, xn_ref, mod_ref, g_ref, win_ref, lw1_ref, lw2_ref, la1_ref, la2_ref,
                 lg1_ref, lg2_ref, mu_rkv_ref, mu_wag_ref, w0_ref, a0_ref, kk_ref, qg_ref, kg_ref,
                 e_ref, cos_ref, sin_ref,
                 r_ref, k_ref, v_ref, kkn_ref, lwf_ref, lwb_ref, af_ref, ab_ref, gg_ref,
                 q_ref, kd_ref, vd_ref, ck_ref, cv_ref, sh_ref):
    i = pl.program_id(0)
    is_lat = i >= NS_CTX
    shift = mod_ref[0, 0:1, :]
    scale = mod_ref[0, 1:2, :]
    gn = g_ref[...]

    def adaln(x):
        return _rms(x) * gn * (1.0 + scale) + shift

    h = adaln(x_ref[...])
    hh = adaln(jnp.concatenate([xp_ref[...], xn_ref[...]], axis=0))
    hb = h.astype(BF16)

    seq = jnp.where(is_lat, L_LAT, L_CTX)
    has_prev = (((i * TS) & (seq - 1)) != 0).astype(F32)
    has_next = ((((i + 1) * TS) & (seq - 1)) != 0).astype(F32)

    def token_shift(z, zh, mix):
        n = z.shape[1]
        dst = sh_ref.at[:, 0:n]
        dst[...] = mix(z, 0.5 * (pltpu.roll(z, 1, 0) + pltpu.roll(z, TS - 1, 0)) - z)

        def fixed(t, nb):
            zt = z[t:t + 1, :]
            return mix(zt, 0.5 * nb - zt)

        dst[0:1, :] = fixed(0, zh[7:8, :] * has_prev + z[1:2, :])
        dst[TS - 1:TS, :] = fixed(TS - 1, z[TS - 2:TS - 1, :] + zh[8:9, :] * has_next)
        for t in range(L_CTX, TS, L_CTX):
            dst[t - 1:t, :] = jnp.where(is_lat, dst[t - 1:t, :], fixed(t - 1, z[t - 2:t - 1, :]))
            dst[t:t + 1, :] = jnp.where(is_lat, dst[t:t + 1, :], fixed(t, z[t + 1:t + 2, :]))
        return dst[...]

    z = _dot(hb, win_ref[:, 0:3 * HALF])
    zq = _dot(hb, win_ref[:, 3 * HALF:4 * HALF])
    zk = _dot(hb, win_ref[:, 4 * HALF:4 * HALF + 256])
    zv = _dot(hb, win_ref[:, 4 * HALF + 256:4 * HALF + 512])
    zh = _dot(hh.astype(BF16), win_ref[:, 0:3 * HALF])
    outs = []
    for c in range(3):
        mu = mu_rkv_ref[c:c + 1, :]
        outs.append(token_shift(z[:, c * HALF:(c + 1) * HALF], zh[:, c * HALF:(c + 1) * HALF],
                                lambda zc, dz: zc + dz * mu))
    r, k, v = outs
    r_ref[...] = r.astype(BF16)
    k_ref[...] = k.astype(BF16)
    v_ref[...] = v.astype(BF16)
    kk = k * kk_ref[...]
    kkn_ref[...] = (kk * lax.rsqrt(_head_sum(kk * kk, e_ref) + 1e-12)).astype(BF16)

    dh = token_shift(h, hh, lambda zc, dz: dz)
    xw = (h + dh * mu_wag_ref[0:1, :]).astype(BF16)
    xa = (h + dh * mu_wag_ref[1:2, :]).astype(BF16)
    xg = (h + dh * mu_wag_ref[2:3, :]).astype(BF16)
    tw = jnp.tanh(_dot(xw, lw1_ref[...])).astype(BF16)
    ta = _dot(xa, la1_ref[...]).astype(BF16)
    for d, (lw_out, a_out) in enumerate(((lwf_ref, af_ref), (lwb_ref, ab_ref))):
        wl = w0_ref[d:d + 1, :] + _dot(tw[:, d * 64:(d + 1) * 64], lw2_ref[d])
        lw_out[...] = (-math.exp(-0.5)) * _sigmoid(wl)
        a_out[...] = _sigmoid(a0_ref[d:d + 1, :] + _dot(ta[:, d * 64:(d + 1) * 64], la2_ref[d])).astype(BF16)
    gg_ref[...] = _dot(_sigmoid(_dot(xg, lg1_ref[...])).astype(BF16), lg2_ref[...]).astype(BF16)

    lat_f = is_lat.astype(F32)
    cos = cos_ref[...] * lat_f + (1.0 - lat_f)
    sin = sin_ref[...] * lat_f
    qn = zq * lax.rsqrt(_head_sum(zq * zq, e_ref) * (1.0 / HD) + EPS) * qg_ref[...]
    cos4 = jnp.concatenate([cos] * NPAIR, axis=1)
    sin4 = jnp.concatenate([sin] * NPAIR, axis=1)
    q_ref[...] = ((qn * cos4 + _swap16(qn) * sin4) * (HD ** -0.5)).astype(BF16)

    kn =zk * lax.rsqrt(_head_sum(zk * zk, e_ref[0:256, 0:256]) * (1.0 / HD) + EPS) * kg_ref[...]
    lane = lax.broadcasted_iota(jnp.int32, (TS, 128), 1)
    ck_ref[...] = jnp.where(lane < HD, kn[:, 0:128], kn[:, 128:256])
    cos2 = jnp.concatenate([cos, cos], axis=1)
    sin2 = jnp.concatenate([sin, sin], axis=1)
    kd_ref[...] = (kn * cos2 + _swap16(kn) * sin2).astype(BF16)
    cv_ref[...] = jnp.where(lane < HD, zv[:, 0:128], zv[:, 128:256])
    vd_ref[...] = zv.astype(BF16)


def _pre_cd(x, mod, g, win, lw1, lw2, la1, la2, lg1, lg2, mu_rkv, mu_wag, w0, a0, k_k, qg, kg, e, cos, sin):
    mi = _mod_index(NS_CTX)
    nblk8 = ROWS // 8
    row512 = lambda i: (i, 0)
    half_out = lambda dt: jax.ShapeDtypeStruct((ROWS, HALF), dt)
    out_shape = [half_out(BF16)] * 4 + [half_out(F32)] * 2 + [half_out(BF16)] * 3 + [
                                 jax.ShapeDtypeStruct((ROWS, HALF), BF16),
                                 jax.ShapeDtypeStruct((ROWS, 256), BF16),
                                 jax.ShapeDtypeStruct((ROWS, 256), BF16),
                                 jax.ShapeDtypeStruct((ROWS, 128), F32),
                                 jax.ShapeDtypeStruct((ROWS, 128), F32)]
    out_specs = [pl.BlockSpec((TS, HALF), row512)] * 10 + [pl.BlockSpec((TS, 256), row512)] * 2 + \
                [pl.BlockSpec((TS, 128), row512)] * 2
    return pl.pallas_call(
        _pre_cd_body,
        out_shape=out_shape,
        grid=(NS,),
        in_specs=[pl.BlockSpec((TS, D), row512),
                  pl.BlockSpec((8, D), lambda i: (jnp.maximum(i * (TS // 8) - 1, 0), 0)),
                  pl.BlockSpec((8, D), lambda i: (jnp.minimum((i + 1) * (TS // 8), nblk8 - 1), 0)),
                  pl.BlockSpec((1, 6, D), lambda i: (mi(i, TS), 0, 0)),
                  _const_spec((1, D)), _const_spec(win.shape),
                  _const_spec(lw1.shape), _const_spec(lw2.shape), _const_spec(la1.shape), _const_spec(la2.shape),
                  _const_spec(lg1.shape), _const_spec(lg2.shape),
                  _const_spec((3, HALF)), _const_spec((3, D)), _const_spec((2, HALF)), _const_spec((2, HALF)),
                  _const_spec((1, HALF)), _const_spec((1, HALF)), _const_spec((1, 256)),
                  _const_spec((HALF, HALF)),
                  pl.BlockSpec((TS, 128), lambda i: (i % (L_LAT // TS), 0)),
                  pl.BlockSpec((TS, 128), lambda i: (i % (L_LAT // TS), 0))],
        out_specs=out_specs,
        scratch_shapes=[pltpu.VMEM((TS, D), F32)],
        compiler_params=_params(("arbitrary",)),
        name="pre_cd",
    )(x, x, x, mod, g, win, lw1, lw2, la1, la2, lg1, lg2, mu_rkv, mu_wag, w0, a0, k_k, qg, kg, e, cos, sin)


def _scan_unit(rc, kc, vc, kkc, lwc, ac, ka, s_prev, reverse, masks):
    cum_mask, tri_s, tri_i, eye, lane_lo, diag8, offs = masks
    lw_hi = lwc.astype(BF16)
    res = lwc - lw_hi.astype(F32)
    lw_mid = res.astype(BF16)
    lw_lo = (res - lw_mid.astype(F32)).astype(BF16)
    cum3 = _dot(cum_mask, jnp.concatenate([lw_hi, lw_mid, lw_lo], axis=1))
    cum = cum3[:, 0:128] + cum3[:, 128:256] + cum3[:, 256:384]
    cume = cum - lwc
    if reverse:
        tot, mid = cum[0:1, :], cum[SC // 2:SC // 2 + 1, :]
    else:
        tot, mid = cum[SC - 1:SC, :], cum[SC // 2 - 1:SC // 2, :]
    at = -kkc * jnp.exp(cume - mid)
    rt = rc * jnp.exp(cum - mid)
    e3 = jnp.exp(mid - cum)
    kt = kc * (1.0 + (ac - 1.0) * ka) * e3
    bt = kkc * ac * e3

    def stack(x):
        return jnp.concatenate([jnp.where(lane_lo, x, 0.0), jnp.where(lane_lo, 0.0, x)], axis=0)

    a2 = stack(at)
    r2 = stack(rt)
    v2 = stack(vc).astype(BF16)
    kb = jnp.concatenate([stack(kt), stack(bt)], axis=0).astype(BF16)
    lhs = jnp.concatenate([a2, r2], axis=0).astype(BF16)
    rhs = jnp.concatenate([bt, bt, kt, kt], axis=0).astype(BF16)
    yield
    p = _dot_nt(lhs, rhs)
    a_ab = jnp.where(tri_s, p[0:128, 0:128], 0.0)
    a_ak = jnp.where(tri_s, p[0:128, 128:256], 0.0)
    a_rb = jnp.where(tri_i, p[128:256, 0:128], 0.0)
    a_rk = jnp.where(tri_i, p[128:256, 128:256], 0.0)

    lk = jnp.where(diag8, a_ab, 0.0)
    tinv = eye + lk
    for _ in range(2):
        lkb = lk.astype(BF16)
        yield
        lk = _dot(lkb, lkb)
        yield
        tinv = _dot(tinv.astype(BF16), (eye + lk).astype(BF16))
    for off in offs:
        tb = tinv.astype(BF16)
        yield
        tl = _dot(tb, jnp.where(off, a_ab, 0.0).astype(BF16)).astype(BF16)
        yield
        tinv = tinv + _dot(tl, tb)

    akv = _dot(a_ak.astype(BF16), v2)
    yield
    wu = _dot(tinv.astype(BF16), jnp.concatenate([a2, akv], axis=1).astype(BF16))
    q = jnp.concatenate([jnp.concatenate([jnp.zeros((128, 128), BF16), v2], axis=1),
                         wu.astype(BF16)], axis=0)
    yield
    ry = _dot(jnp.concatenate([a_rk, a_rb], axis=1).astype(BF16), q)
    rw2 = r2 + ry[:, 0:128]
    y02 = ry[:, 128:256]
    gh = _dot_tn(q, kb)

    sd = s_prev * jnp.exp(mid)
    sdb = sd.astype(BF16)
    yield
    ys = _dot_nt(rw2.astype(BF16), sdb)
    sg = _dot(sdb, gh[0:128, :].astype(BF16))
    s_new = s_prev * jnp.exp(tot) + (sg + gh[128:256, :]) * jnp.exp(tot - mid)
    y2 = y02 + ys
    return y2[0:SC, :] + y2[SC:2 * SC, :], s_new


def _run_lockstep(gens):
    results = [None] * len(gens)
    live = list(range(len(gens)))
    while live:
        still = []
        for u in live:
            try:
                next(gens[u])
                still.append(u)
            except StopIteration as stop:
                results[u] = stop.value
        live = still
    return results


def _scan_body(tab_ref, *refs):
    n_in = NMEM * 2 * 6
    in_refs = refs[:n_in]
    ka_ref = refs[n_in]
    s0_refs = refs[n_in + 1:n_in + 1 + 2 * NMEM]
    yf_ref, yb_ref, sf_ref, sb_ref, st_ref = refs[n_in + 1 + 2 * NMEM:]
    t = pl.program_id(0)
    first = tab_ref[_T_FIRST, t]
    last = tab_ref[_T_LAST, t]
    is_lat = tab_ref[_T_LAT, t]

    def unit(m, d, p):
        return (m * 2 + d) * NPAIR + p

    @pl.when(first == 1)
    def _():
        zero = jnp.zeros((HD, HD), F32)
        lat_f = is_lat.astype(F32)
        for m in range(NMEM):
            for d in range(2):
                for p in range(NPAIR):
                    lo = s0_refs[m * 2 + d][0, 2 * p] * lat_f
                    hi = s0_refs[m * 2 + d][0, 2 * p + 1] * lat_f
                    st_ref[unit(m, d, p)] = jnp.concatenate(
                        [jnp.concatenate([lo, zero], axis=1), jnp.concatenate([zero, hi], axis=1)], axis=0)

    ri = lax.broadcasted_iota(jnp.int32, (SC, SC), 0)
    ci = lax.broadcasted_iota(jnp.int32, (SC, SC), 1)
    r2 = lax.broadcasted_iota(jnp.int32, (128, 128), 0)
    c2 = lax.broadcasted_iota(jnp.int32, (128, 128), 1)
    same = (r2 & SC) == (c2 & SC)
    eye = (r2 == c2).astype(F32)
    diag8 = (r2 >> 3) == (c2 >> 3)
    lane_lo = lax.broadcasted_iota(jnp.int32, (SC, 128), 1) < HD

    gens, dests = [], []
    for d, reverse in enumerate((False, True)):
        early, late = (r2, c2) if reverse else (c2, r2)
        offs = tuple(((r2 >> sh + 1) == (c2 >> sh + 1)) & ((late & (1 << sh)) != 0) & ((early & (1 << sh)) == 0)
                     for sh in (3, 4, 5))
        if reverse:
            masks = ((ri <= ci).astype(BF16), same & (r2 < c2), same & (r2 <= c2), eye, lane_lo, diag8, offs)
        else:
            masks = ((ri >= ci).astype(BF16), same & (r2 > c2), same & (r2 >= c2), eye, lane_lo, diag8, offs)
        y_ref = yb_ref if reverse else yf_ref
        for m in range(NMEM):
            unit_refs = in_refs[(m * 2 + d) * 6:(m * 2 + d + 1) * 6]
            for p in range(NPAIR):
                ls = slice(p * 128, (p + 1) * 128)
                rc, kc, vc, kkc, lwc, ac = (ref[:, ls].astype(F32) for ref in unit_refs)
                gens.append(_scan_unit(rc, kc, vc, kkc, lwc, ac, ka_ref[:, ls], st_ref[unit(m, d, p)],
                                       reverse, masks))
                dests.append((y_ref, m, ls, unit(m, d, p)))
    for (y_ref, m, ls, u), (y, s_new) in zip(dests, _run_lockstep(gens)):
        y_ref[m, 0, :, ls] = y
        st_ref[u] = s_new

    @pl.when(last == 1)
    def _():
        for m in range(NMEM):
            for d, s_out in enumerate((sf_ref, sb_ref)):
                for p in range(NPAIR):
                    s = st_ref[unit(m, d, p)]
                    s_out[m, 0, 2 * p] = s[0:HD, 0:HD]
                    s_out[m, 0, 2 * p + 1] = s[HD:2 * HD, HD:2 * HD]


_T_FWD, _T_BWD = 0, NMEM
_T_YF, _T_YB = 2 * NMEM, 2 * NMEM + 1
_T_FIRST, _T_LAST, _T_LAT = 2 * NMEM + 2, 2 * NMEM + 3, 2 * NMEM + 4
_T_SIN = 2 * NMEM + 5
_T_SOUT = 3 * NMEM + 5
_Y_SLOTS_CTX = ROWS_CTX // SC // NMEM


def _scan_tables():
    cols = []
    for group, (nb, ln, base, slot0) in enumerate(((N_CTX, L_CTX, 0, 0), (N_LAT, L_LAT, ROWS_CTX, _Y_SLOTS_CTX))):
        nc = ln // SC
        for p in range(nb // NMEM):
            batches = [p + m * (nb // NMEM) for m in range(NMEM)]
            blks = [(base + b * ln) // SC for b in batches]
            for c in range(nc):
                col = [blk + c for blk in blks] + [blk + nc - 1 - c for blk in blks]
                col += [slot0 + p * nc + c, slot0 + p * nc + nc - 1 - c]
                col += [int(c == 0), int(c == nc - 1), group]
                col += [b if group == 1 else 0 for b in batches]
                col += [p if group == 0 else N_CTX // NMEM]
                cols.append(col)
    return np.asarray(cols, np.int32).T


def _scan(r, k, v, kk, lwf, lwb, af, ab, k_a, s0f, s0b):
    tab = _scan_tables()
    steps = tab.shape[1]

    def row_spec(table_row):
        return pl.BlockSpec((SC, HALF), lambda t, tab: (tab[table_row, t], 0))

    def state_spec(table_row):
        return pl.BlockSpec((1, 2 * NPAIR, HD, HD), lambda t, tab: (tab[table_row, t], 0, 0, 0))

    in_specs, args = [], []
    for m in range(NMEM):
        for table_row, lw, a in ((_T_FWD + m, lwf, af), (_T_BWD + m, lwb, ab)):
            in_specs += [row_spec(table_row)] * 6
            args += [r, k, v, kk, lw, a]
    in_specs.append(pl.BlockSpec((1, HALF), lambda t, tab: (0, 0)))
    args.append(k_a)
    for m in range(NMEM):
        in_specs += [state_spec(_T_SIN + m)] * 2
        args += [s0f, s0b]
    y_shape = jax.ShapeDtypeStruct((NMEM, ROWS // SC // NMEM, SC, HALF), F32)
    s_shape = jax.ShapeDtypeStruct((NMEM, N_CTX // NMEM + 1, 2 * NPAIR, HD, HD), F32)
    s_out = pl.BlockSpec((NMEM, 1, 2 * NPAIR, HD, HD), lambda t, tab: (0, tab[_T_SOUT, t], 0, 0, 0))
    grid_spec = pltpu.PrefetchScalarGridSpec(
        num_scalar_prefetch=1,
        grid=(steps,),
        in_specs=in_specs,
        out_specs=[pl.BlockSpec((NMEM, 1, SC, HALF), lambda t, tab: (0, tab[_T_YF, t], 0, 0)),
                   pl.BlockSpec((NMEM, 1, SC, HALF), lambda t, tab: (0, tab[_T_YB, t], 0, 0)),
                   s_out, s_out],
        scratch_shapes=[pltpu.VMEM((NMEM * 2 * NPAIR, 128, 128), F32)],
    )
    return pl.pallas_call(
        _scan_body,
        out_shape=[y_shape, y_shape, s_shape, s_shape],
        grid_spec=grid_spec,
        compiler_params=_params(("arbitrary",)),
        name="wkv_scan",
    )(jnp.asarray(tab), *args)


def _attn_body(q_ref, k_ref, v_ref, *rest, cached):
    if cached:
        ck_ref, cv_ref, o_ref = rest
    else:
        (o_ref,) = rest
    lane_lo = lax.broadcasted_iota(jnp.int32, (QB, 128), 1) < HD

    def head(pr, half):
        qp = q_ref[:, pr * 128:(pr + 1) * 128]
        kvh = (2 * pr) // 4
        ks = slice(kvh * 128, (kvh + 1) * 128)
        qh = jnp.where(lane_lo if half == 0 else jnp.logical_not(lane_lo), qp, jnp.zeros_like(qp))
        s_new = _dot_nt(qh, k_ref[:, ks])
        if cached:
            s_old = _dot_nt(qh, ck_ref[0, :, ks])
            yield
            m = jnp.maximum(jnp.max(s_new, -1, keepdims=True), jnp.max(s_old, -1, keepdims=True))
            p_new = jnp.exp(s_new - m)
            p_old = jnp.exp(s_old - m)
            den = jnp.sum(p_new, -1, keepdims=True) + jnp.sum(p_old, -1, keepdims=True)
            yield
            o = _dot(p_new.astype(BF16), v_ref[:, ks]) + _dot(p_old.astype(BF16), cv_ref[0, :, ks])
        else:
            yield
            m = jnp.max(s_new, -1, keepdims=True)
            p_new = jnp.exp(s_new - m)
            den = jnp.sum(p_new, -1, keepdims=True)
            yield
            o = _dot(p_new.astype(BF16), v_ref[:, ks])
        return o / den

    outs = _run_lockstep([head(pr, half) for pr in range(NPAIR) for half in range(2)])
    for pr in range(NPAIR):
        o_ref[:, pr * 128:(pr + 1) * 128] = jnp.where(lane_lo, outs[2 * pr], outs[2 * pr + 1]).astype(BF16)


def _attention(q, kd, vd, base, nb, ln, cache=None):
    nq = ln // QB
    blk0 = base // QB
    seq0 = base // ln
    in_specs = [pl.BlockSpec((QB, HALF), lambda b, j: (blk0 + b * nq + j, 0)),
                pl.BlockSpec((ln, 256), lambda b, j: (seq0 + b, 0)),
                pl.BlockSpec((ln, 256), lambda b, j: (seq0 + b, 0))]
    args = [q, kd, vd]
    if cache is not None:
        in_specs += [pl.BlockSpec((1, PAST, 256), lambda b, j: (b, 0, 0))] * 2
        args += list(cache)
    return pl.pallas_call(
        functools.partial(_attn_body, cached=cache is not None),
        out_shape=jax.ShapeDtypeStruct((nb * ln, HALF), BF16),
        grid=(nb, nq),
        in_specs=in_specs,
        out_specs=pl.BlockSpec((QB, HALF), lambda b, j: (b * nq + j, 0)),
        compiler_params=_params(("arbitrary", "arbitrary")),
        name="gqa_cached" if cache is not None else "gqa",
    )(*args)


def _post_cd_body(x_ref, mod_ref, yf_ref, yb_ref, r_ref, k_ref, v_ref, af_ref, ab_ref, gg_ref,
                  atc_ref, atl_ref, wout_ref, e_ref, lng_ref, lnb_ref, rk_ref, ka_ref, o_ref):
    i = pl.program_id(0)
    gate = mod_ref[0, 2:3, :]
    y = yf_ref[0].reshape(TS, HALF) + yb_ref[0].reshape(TS, HALF)
    yc = y - _head_sum(y, e_ref) * (1.0 / HD)
    yn = yc * lax.rsqrt(_head_sum(yc * yc, e_ref) * (1.0 / HD) + GN_EPS) * lng_ref[...] + lnb_ref[...]
    f32 = lambda ref: ref[...].astype(F32)
    kd2 = f32(k_ref) * (2.0 + (f32(af_ref) + f32(ab_ref) - 2.0) * ka_ref[...])
    bonus = _head_sum(f32(r_ref) * kd2 * rk_ref[...], e_ref) * f32(v_ref)
    yc_out = ((yn + bonus) * f32(gg_ref)).astype(BF16)
    att = jnp.where(i >= NS_CTX, atl_ref[...], atc_ref[...])
    out = _dot(yc_out, wout_ref[0:HALF, :]) + _dot(att, wout_ref[HALF:2 * HALF, :])
    o_ref[...] = x_ref[...] + gate * out


def _post_cd(x, mod, yf, yb, r, k, v, af, ab, gg, att_c, att_l, wout, e, lng, lnb, r_k, k_a):
    mi = _mod_index(NS_CTX)
    row = lambda i: (i, 0)
    half = pl.BlockSpec((TS, HALF), row)
    ctx_per_mem = NS_CTX // NMEM
    lat_per_mem = (NS - NS_CTX) // NMEM

    def y_index(i):
        j = i - NS_CTX
        return (jnp.where(i < NS_CTX, i // ctx_per_mem, j // lat_per_mem),
                jnp.where(i < NS_CTX, i % ctx_per_mem, ctx_per_mem + j % lat_per_mem), 0, 0)

    y_spec = pl.BlockSpec((1, TS // SC, SC, HALF), y_index)
    return pl.pallas_call(
        _post_cd_body,
        out_shape=jax.ShapeDtypeStruct((ROWS, D), F32),
        grid=(NS,),
        in_specs=[pl.BlockSpec((TS, D), row),
                  pl.BlockSpec((1, 6, D), lambda i: (mi(i, TS), 0, 0)), y_spec, y_spec] + [half] * 6 +
                 [pl.BlockSpec((TS, HALF), lambda i: (jnp.minimum(i, NS_CTX - 1), 0)),
                  pl.BlockSpec((TS, HALF), lambda i: (jnp.maximum(i - NS_CTX, 0), 0)),
                  _layer_spec((D, D), 1), _const_spec((HALF, HALF)),
                  _const_spec((1, HALF)), _const_spec((1, HALF)), _const_spec((1, HALF)), _const_spec((1, HALF))],
        out_specs=pl.BlockSpec((TS, D), row),
        compiler_params=_params(("arbitrary",)),
        name="post_cd",
    )(x, mod, yf, yb, r, k, v, af, ab, gg, att_c, att_l, wout, e, lng, lnb, r_k, k_a)


def _rope_tables():
    half = 16
    inv = 10000.0 ** (-jnp.arange(half, dtype=F32) / half)
    t = jnp.arange(L_LAT)
    cos_parts, sin_parts = [], []
    for pos in (t // GRID_W, t % GRID_W):
        ang = pos.astype(F32)[:, None] * inv[None, :]
        cos_parts += [jnp.cos(ang), jnp.cos(ang)]
        sin_parts += [-jnp.sin(ang), jnp.sin(ang)]
    cos = jnp.concatenate(cos_parts * 2, axis=1)
    sin = jnp.concatenate(sin_parts * 2, axis=1)
    return cos, sin


def kernel(x_prompt, x_sample, cache_k, cache_v, state_wkv_fwd, state_wkv_bwd, c, c_ctx, mod_w, mod_b, norm1_g, norm2_g, w_out, ffn_up, ffn_conv_w, ffn_conv_b, ffn_down, final_g, ab_w_in, a_vnorm_g, a_vnorm_b, a_ws, a_bs, b_conv_w, b_conv_b, b_norm_g, cd_w_in, c_mu_rkv, c_mu_wag, c_w0, c_w1, c_w2, c_a0, c_a1, c_a2, c_g1, c_g2, c_k_k, c_k_a, c_r_k, c_ln_g, c_ln_b, d_q_g, d_k_g):
    row = lambda a: a.reshape(1, -1)
    cvec =jnp.concatenate([c_ctx[None, :], c, jnp.zeros((16 - 1 - N_LAT, D), F32)], axis=0)
    mods = _modulation(cvec, mod_w, mod_b).reshape(2, 16, 6, D)

    bs = jnp.broadcast_to(a_bs[0][:, :, None], (A_GROUPS, CHUNK, 128))
    cw = jnp.concatenate([b_conv_w[0], jnp.zeros((1, HALF), F32)], axis=0)
    w_out_b, ffn_up_b, ffn_down_b = w_out.astype(BF16), ffn_up.astype(BF16), ffn_down.astype(BF16)
    x = _mix_ab(x_prompt.reshape(ROWS_CTX, D), x_sample.reshape(ROWS_LAT, D), mods[0], row(norm1_g[0]),
                ab_w_in[0].astype(BF16), w_out_b,
                a_ws[0].astype(BF16), bs, row(a_vnorm_g[0]), row(a_vnorm_b[0]),
                cw, row(b_conv_b[0]), row(b_norm_g[0]))
    x = _ffn(x, 0, NT, 0, mods[0], row(norm2_g[0]), ffn_up_b, ffn_conv_w, row(ffn_conv_b[0]),
             ffn_down_b, row(final_g), final=False)

    w = cd_w_in[0]
    wk = w[:, 4 * HALF:4 * HALF + 128]
    wv = w[:, 4 * HALF + 128:4 * HALF + 256]
    win = jnp.concatenate([w[:, :4 * HALF], wk[:, :HD], wk[:, :HD], wk[:, HD:], wk[:, HD:],
                           wv[:, :HD], wv[:, :HD], wv[:, HD:], wv[:, HD:]], axis=1).astype(BF16)
    lw1 = jnp.concatenate([c_w1[0, 0], c_w1[0, 1]], axis=1).astype(BF16)
    la1 = jnp.concatenate([c_a1[0, 0], c_a1[0, 1]], axis=1).astype(BF16)
    ones_blk = jnp.asarray(np.kron(np.eye(HALF // HD), np.ones((HD, HD))), BF16)
    cos, sin = _rope_tables()
    k_a = row(c_k_a[0])
    (r, k, v, kk, lwf, lwb, af, ab, gg, q, kd, vd, ck, cv) = _pre_cd(
        x, mods[1], row(norm1_g[1]), win, lw1, c_w2[0].astype(BF16), la1, c_a2[0].astype(BF16),
        c_g1[0].astype(BF16), c_g2[0].astype(BF16), c_mu_rkv[0], c_mu_wag[0], c_w0[0], c_a0[0],
        row(c_k_k[0]), jnp.tile(d_q_g[0], HALF // HD)[None, :], jnp.tile(d_k_g[0], 4)[None, :],
        ones_blk, cos, sin)

    yf, yb, s_f, s_b = _scan(r, k, v, kk, lwf, lwb, af, ab, k_a, state_wkv_fwd[:, 0], state_wkv_bwd[:, 0])

    def dup_cache(t):
        t = t[:, 0].astype(BF16)
        return jnp.concatenate([t[:, :, 0], t[:, :, 0], t[:, :, 1], t[:, :, 1]], axis=-1)

    att_c = _attention(q, kd, vd, 0, N_CTX, L_CTX)
    att_l = _attention(q, kd, vd, ROWS_CTX, N_LAT, L_LAT, cache=(dup_cache(cache_k), dup_cache(cache_v)))

    x = _post_cd(x, mods[1], yf, yb, r, k, v, af, ab, gg, att_c, att_l, w_out_b, ones_blk,
                 row(c_ln_g[0]), row(c_ln_b[0]), row(c_r_k[0]), k_a)
    ffn1 = (1, mods[1], row(norm2_g[1]), ffn_up_b, ffn_conv_w, row(ffn_conv_b[1]), ffn_down_b, row(final_g))
    y_prompt = _ffn(x, 0, NT_CTX, *ffn1, final=True).reshape(N_CTX, L_CTX, D)
    y_sample = _ffn(x, NT_CTX, NT - NT_CTX, *ffn1, final=True).reshape(N_LAT, L_LAT, D)
    new_k = ck[:ROWS_CTX].reshape(N_CTX, 1, L_CTX, 2, HD)
    new_v = cv[:ROWS_CTX].reshape(N_CTX, 1, L_CTX, 2, HD)
    return (y_prompt, y_sample, new_k, new_v,
            s_f[:, :N_CTX // NMEM].reshape(N_CTX, 1, 2 * NPAIR, HD, HD),
            s_b[:, :N_CTX // NMEM].reshape(N_CTX, 1, 2 * NPAIR, HD, HD))
```

```python
import functools
import math

import jax
import jax.numpy as jnp
import numpy as np
from jax import lax
from jax.experimental import pallas as pl
from jax.experimental.pallas import tpu as pltpu

F32 = jnp.float32
BF16 = jnp.bfloat16

D = 1024
N_CTX, L_CTX = 16, 256
N_LAT, L_LAT = 8, 1024
ROWS_CTX = N_CTX * L_CTX
ROWS_LAT = N_LAT * L_LAT
ROWS = ROWS_CTX + ROWS_LAT
PAST = 256
GRID_W = 64
EPS = 1e-6
GN_EPS = 64e-5
HALF = 512
HD = 64
NPAIR = HALF // 128
D_FF = 2816
B_CONV = 31
CHUNK = 128
A_GROUPS = 4

TM = 1024
NT = ROWS // TM
NT_CTX = ROWS_CTX // TM
SEG = 256
NSEG = TM // SEG
HALO = 16
TS = 512
NS = ROWS // TS
NS_CTX = ROWS_CTX // TS
FB = 256
NJ = D_FF // FB
SC = 64
SUB = 2
SCB = SC * SUB
NMEM = 2
QB = 256
VMEM_LIMIT = 56 * 1024 * 1024

_NT_DIMS = (((1,), (1,)), ((), ()))
_TN_DIMS = (((0,), (0,)), ((), ()))


def _dot(a, b):
    return jnp.dot(a, b, preferred_element_type=F32)


def _dot_nt(a, b):
    return lax.dot_general(a, b, _NT_DIMS, preferred_element_type=F32)


def _dot_tn(a, b):
    return lax.dot_general(a, b, _TN_DIMS, preferred_element_type=F32)


def _rms(x, eps=EPS):
    return x * lax.rsqrt(jnp.mean(x * x, -1, keepdims=True) + eps)


def _sigmoid(x):
    return 1.0 / (1.0 + jnp.exp(-x))


def _silu(x):
    return x * _sigmoid(x)


def _gelu_tanh(x):
    return 0.5 * x * (1.0 + jnp.tanh(math.sqrt(2.0 / math.pi) * (x + 0.044715 * (x * x * x))))


def _const_spec(shape):
    return pl.BlockSpec(shape, lambda *_: (0,) * len(shape), pipeline_mode=pl.Buffered(1))


def _layer_spec(shape, layer):
    return pl.BlockSpec((None,) + tuple(shape), lambda *_: (layer,) + (0,) * len(shape), pipeline_mode=pl.Buffered(1))


def _params(sem):
    return pltpu.CompilerParams(dimension_semantics=sem, vmem_limit_bytes=VMEM_LIMIT)


def _mod_body(c_ref, w_ref, b_ref, o_ref):
    s = _silu(c_ref[...])
    o_ref[0] = jnp.dot(s, w_ref[0], preferred_element_type=F32,
                       precision=lax.Precision.HIGHEST) + b_ref[0]


def _modulation(cvec, mod_w, mod_b):
    depth = mod_w.shape[0]
    nb = 1536
    return pl.pallas_call(
        _mod_body,
        out_shape=jax.ShapeDtypeStruct((depth, 16, 6 * D), F32),
        grid=(depth, 6 * D // nb),
        in_specs=[pl.BlockSpec((16, D), lambda l, j: (0, 0)),
                  pl.BlockSpec((1, D, nb), lambda l, j: (l, 0, j)),
                  pl.BlockSpec((1, 1, nb), lambda l, j: (l, 0, j))],
        out_specs=pl.BlockSpec((1, 16, nb), lambda l, j: (l, 0, j)),
        compiler_params=_params(("arbitrary", "arbitrary")),
        name="modulation",
    )(cvec, mod_w, mod_b.reshape(depth, 1, 6 * D))


def _mod_index(tiles_ctx):
    def idx(i, tile_rows):
        return jnp.where(i >= tiles_ctx, (i - tiles_ctx) // (L_LAT // tile_rows) + 1, 0)
    return idx


def _mix_ab_body(xc_ref, xl_ref, mod_ref, g_ref, win_ref, wout_ref, ws_ref, bs_ref, vg_ref, vb_ref,
                 cw_ref, cb_ref, ng_ref, o_ref, cat_ref, pad_ref, shift_ref):
    i = pl.program_id(0)
    is_lat = i >= NT_CTX
    lat = is_lat.astype(F32)

    def x_rows(r0):
        return jnp.where(is_lat, xl_ref[pl.ds(r0, SEG), :], xc_ref[pl.ds(r0, SEG), :])

    shift = mod_ref[0, 0:1, :]
    scale = mod_ref[0, 1:2, :]
    gate = mod_ref[0, 2:3, :]

    def phase1(s, carry):
        r0 = pl.multiple_of(s * SEG, SEG)
        x = x_rows(r0)
        h = (_rms(x) * g_ref[...] * (1.0 + scale) + shift).astype(BF16)
        z = _dot(h, win_ref[...])
        za = _gelu_tanh(z[:, :2 * HALF])
        u = za[:, :HALF]
        v = za[:, HALF:]
        vc = v - jnp.mean(v, -1, keepdims=True)
        vn = vc * lax.rsqrt(jnp.mean(vc * vc, -1, keepdims=True) + EPS) * vg_ref[...] + vb_ref[...]
        vnb = vn.astype(BF16)
        for n in range(SEG // CHUNK):
            parts = []
            for g in range(A_GROUPS):
                vv = vnb[n * CHUNK:(n + 1) * CHUNK, g * 128:(g + 1) * 128]
                parts.append(_dot(ws_ref[g], vv) + bs_ref[g])
            vp = jnp.concatenate(parts, axis=1)
            ya = u[n * CHUNK:(n + 1) * CHUNK, :] * vp
            cat_ref[pl.ds(pl.multiple_of(r0 + n * CHUNK, CHUNK), CHUNK), 0:HALF] = ya.astype(BF16)
        zb = z[:, 2 * HALF:]
        pad_ref[s, HALO:HALO + SEG, :] = zb[:, :HALF] * _sigmoid(zb[:, HALF:])
        return carry

    lax.fori_loop(0, NSEG, phase1, 0)

    zeros_h = jnp.zeros((HALO, HALF), F32)
    for s in range(NSEG):
        left = pad_ref[s - 1, SEG:SEG + HALO, :] * lat if s > 0 else zeros_h
        right = pad_ref[s + 1, HALO:2 * HALO, :] * lat if s < NSEG - 1 else zeros_h
        pad_ref[s, 0:HALO, :] = left
        pad_ref[s, HALO + SEG:2 * HALO + SEG, :] = right

    RB = 32

    shift_rows = SEG + 2 * HALO - 8

    def phase3(s, carry):
        r0 = pl.multiple_of(s * SEG, SEG)
        for sh in range(1, 8):
            shift_ref[sh - 1] = pad_ref[s, sh:sh + shift_rows, :]
        for rb in range(SEG // RB):
            acc = jnp.zeros((RB, HALF), F32) + cb_ref[...]
            for k in range(B_CONV):
                q8, sh = divmod(HALO - B_CONV // 2 + k, 8)
                off = 8 * q8 + rb * RB
                win = pad_ref[s, off:off + RB, :] if sh == 0 else shift_ref[sh - 1, off:off + RB, :]
                acc = acc + cw_ref[k:k + 1, :] * win
            yb = _silu(_rms(acc) * ng_ref[...])
            cat_ref[pl.ds(pl.multiple_of(r0 + rb * RB, RB), RB), HALF:2 * HALF] = yb.astype(BF16)
        out = _dot(cat_ref[pl.ds(r0, SEG), :], wout_ref[...])
        o_ref[pl.ds(r0, SEG), :] = x_rows(r0) + gate * out
        return carry

    lax.fori_loop(0, NSEG, phase3, 0)


def _mix_ab(xc, xl, mod, g, win, wout, ws, bs, vg, vb, cw, cb, ng):
    mi = _mod_index(NT_CTX)
    return pl.pallas_call(
        _mix_ab_body,
        out_shape=jax.ShapeDtypeStruct((ROWS, D), F32),
        grid=(NT,),
        in_specs=[pl.BlockSpec((TM, D), lambda i: (jnp.minimum(i, NT_CTX - 1), 0)),
                  pl.BlockSpec((TM, D), lambda i: (jnp.maximum(i - NT_CTX, 0), 0)),
                  pl.BlockSpec((1, 6, D), lambda i: (mi(i, TM), 0, 0)),
                  _const_spec((1, D)), _const_spec((D, 4 * HALF)), _layer_spec((D, D), 0),
                  _const_spec((A_GROUPS, CHUNK, CHUNK)), _const_spec((A_GROUPS, CHUNK, 128)),
                  _const_spec((1, HALF)), _const_spec((1, HALF)),
                  _const_spec((B_CONV + 1, HALF)), _const_spec((1, HALF)), _const_spec((1, HALF))],
        out_specs=pl.BlockSpec((TM, D), lambda i: (i, 0)),
        scratch_shapes=[pltpu.VMEM((TM, D), BF16),
                        pltpu.VMEM((NSEG, SEG + 2 * HALO, HALF), F32),
                        pltpu.VMEM((7, SEG + 2 * HALO - 8, HALF), F32)],
        compiler_params=_params(("arbitrary",)),
        name="mix_ab",
    )(xc, xl, mod, g, win, wout, ws, bs, vg, vb, cw, cb, ng)


def _ffn_body(x_ref, mod_ref, g_ref, up_ref, cw_ref, cb_ref, down_ref, fg_ref, o_ref,
              h_ref, gated_ref, cza_ref, czb_ref, *, tile0, final):
    is_ctx = tile0 + pl.program_id(0) < NT_CTX
    shift = mod_ref[0, 3:4, :]
    scale = mod_ref[0, 4:5, :]
    h_ref[...] = (_rms(x_ref[...]) * g_ref[...] * (1.0 + scale) + shift).astype(BF16)

    def up_proj(col):
        return _dot(h_ref[...], up_ref[:, col * FB:(col + 1) * FB])

    def conv_branch(col, z, cz_ref):
        cs = slice(col * FB, (col + 1) * FB)
        w0, w1, w2, b = cw_ref[0:1, cs], cw_ref[1:2, cs], cw_ref[2:3, cs], cb_ref[:, cs]
        cz_ref[...] = w0 * pltpu.roll(z, 1, 0) + w1 * z + w2 * pltpu.roll(z, TM - 1, 0) + b
        for t0 in range(0, TM, L_CTX):
            t1 = t0 + L_CTX - 1
            first = w1 * z[t0:t0 + 1, :] + w2 * z[t0 + 1:t0 + 2, :] + b
            last = w0 * z[t1 - 1:t1, :] + w1 * z[t1:t1 + 1, :] + b
            cz_ref[t0:t0 + 1, :] = first if t0 == 0 else jnp.where(is_ctx, first, cz_ref[t0:t0 + 1, :])
            cz_ref[t1:t1 + 1, :] = last if t1 == TM - 1 else jnp.where(is_ctx, last, cz_ref[t1:t1 + 1, :])

    for j in range(NJ):
        conv_branch(j, up_proj(j), cza_ref)
        conv_branch(NJ + j, up_proj(NJ + j), czb_ref)
        gated_ref[:, j * FB:(j + 1) * FB] = (_silu(czb_ref[...]) * cza_ref[...]).astype(BF16)

    y = x_ref[...] + mod_ref[0, 5:6, :] * _dot(gated_ref[...], down_ref[...])
    if final:
        y = _rms(y) * fg_ref[...]
    o_ref[...] = y


def _ffn(x, tile0, ntiles, layer, mod, g, up, cw, cb, down, fg, final):
    mi = _mod_index(NT_CTX)
    return pl.pallas_call(
        functools.partial(_ffn_body, tile0=tile0, final=final),
        out_shape=jax.ShapeDtypeStruct((ntiles * TM, D), F32),
        grid=(ntiles,),
        in_specs=[pl.BlockSpec((TM, D), lambda i: (tile0 + i, 0)),
                  pl.BlockSpec((1, 6, D), lambda i: (mi(tile0 + i, TM), 0, 0)),
                  _const_spec((1, D)),
                  _layer_spec((D, 2 * D_FF), layer), _layer_spec((3, 2 * D_FF), layer), _const_spec((1, 2 * D_FF)),
                  _layer_spec((D_FF, D), layer), _const_spec((1, D))],
        out_specs=pl.BlockSpec((TM, D), lambda i: (i, 0)),
        scratch_shapes=[pltpu.VMEM((TM, D), BF16), pltpu.VMEM((TM, D_FF), BF16),
                        pltpu.VMEM((TM, FB), F32), pltpu.VMEM((TM, FB), F32)],
        compiler_params=_params(("arbitrary",)),
        name="conv_ffn",
    )(x, mod, g, up, cw, cb, down, fg)


def _head_sum(x, e_ref):
    return _dot(x.astype(BF16), e_ref[...])


def _swap16(x):
    lane = lax.broadcasted_iota(jnp.int32, x.shape, 1)
    n = x.shape[1]
    return jnp.where((lane & 16) == 0, pltpu.roll(x, n - 16, 1), pltpu.roll(x, 16, 1))


def _pre_cd_body(x_ref, xp_ref, xn_ref, mod_ref, g_ref, win_ref, lw1_ref, lw2_ref, la1_ref, la2_ref,
                 lg1_ref, lg2_ref, mu_rkv_ref, mu_wag_ref, w0_ref, a0_ref, kk_ref, qg_ref, kg_ref,
                 e_ref, cos_ref, sin_ref,
                 r_ref, k_ref, v_ref, kkn_ref, lwf_ref, lwb_ref, af_ref, ab_ref, gg_ref,
                 q_ref, kd_ref, vd_ref, ck_ref, cv_ref, sh_ref):
    i = pl.program_id(0)
    is_lat = i >= NS_CTX
    shift = mod_ref[0, 0:1, :]
    scale = mod_ref[0, 1:2, :]
    gn = g_ref[...]

    def adaln(x):
        return _rms(x) * gn * (1.0 + scale) + shift

    h = adaln(x_ref[...])
    hh = adaln(jnp.concatenate([xp_ref[...], xn_ref[...]], axis=0))
    hb = h.astype(BF16)

    seq = jnp.where(is_lat, L_LAT, L_CTX)
    has_prev = (((i * TS) & (seq - 1)) != 0).astype(F32)
    has_next = ((((i + 1) * TS) & (seq - 1)) != 0).astype(F32)

    def token_shift(z, zh, mix):
        n = z.shape[1]
        dst = sh_ref.at[:, 0:n]
        dst[...] = mix(z, 0.5 * (pltpu.roll(z, 1, 0) + pltpu.roll(z, TS - 1, 0)) - z)

        def fixed(t, nb):
            zt = z[t:t + 1, :]
            return mix(zt, 0.5 * nb - zt)

        dst[0:1, :] = fixed(0, zh[7:8, :] * has_prev + z[1:2, :])
        dst[TS - 1:TS, :] = fixed(TS - 1, z[TS - 2:TS - 1, :] + zh[8:9, :] * has_next)
        for t in range(L_CTX, TS, L_CTX):
            dst[t - 1:t, :] = jnp.where(is_lat, dst[t - 1:t, :], fixed(t - 1, z[t - 2:t - 1, :]))
            dst[t:t + 1, :] = jnp.where(is_lat, dst[t:t + 1, :], fixed(t, z[t + 1:t + 2, :]))
        return dst[...]

    z = _dot(hb, win_ref[:, 0:3 * HALF])
    zq = _dot(hb, win_ref[:, 3 * HALF:4 * HALF])
    zk = _dot(hb, win_ref[:, 4 * HALF:4 * HALF + 256])
    zv = _dot(hb, win_ref[:, 4 * HALF + 256:4 * HALF + 512])
    zh = _dot(hh.astype(BF16), win_ref[:, 0:3 * HALF])
    outs = []
    for c in range(3):
        mu = mu_rkv_ref[c:c + 1, :]
        outs.append(token_shift(z[:, c * HALF:(c + 1) * HALF], zh[:, c * HALF:(c + 1) * HALF],
                                lambda zc, dz: zc + dz * mu))
    r, k, v = outs
    r_ref[...] = r.astype(BF16)
    k_ref[...] = k.astype(BF16)
    v_ref[...] = v.astype(BF16)
    kk = k * kk_ref[...]
    kkn_ref[...] = (kk * lax.rsqrt(_head_sum(kk * kk, e_ref) + 1e-12)).astype(BF16)

    dh = token_shift(h, hh, lambda zc, dz: dz)
    xw = (h + dh * mu_wag_ref[0:1, :]).astype(BF16)
    xa = (h + dh * mu_wag_ref[1:2, :]).astype(BF16)
    xg = (h + dh * mu_wag_ref[2:3, :]).astype(BF16)
    tw = jnp.tanh(_dot(xw, lw1_ref[...])).astype(BF16)
    ta = _dot(xa, la1_ref[...]).astype(BF16)
    for d, (lw_out, a_out) in enumerate(((lwf_ref, af_ref), (lwb_ref, ab_ref))):
        wl = w0_ref[d:d + 1, :] + _dot(tw[:, d * 64:(d + 1) * 64], lw2_ref[d])
        lw_out[...] = (-math.exp(-0.5)) * _sigmoid(wl)
        a_out[...] = _sigmoid(a0_ref[d:d + 1, :] + _dot(ta[:, d * 64:(d + 1) * 64], la2_ref[d])).astype(BF16)
    gg_ref[...] = _dot(_sigmoid(_dot(xg, lg1_ref[...])).astype(BF16), lg2_ref[...]).astype(BF16)

    lat_f = is_lat.astype(F32)
    cos = cos_ref[...] * lat_f + (1.0 - lat_f)
    sin = sin_ref[...] * lat_f
    qn = zq * lax.rsqrt(_head_sum(zq * zq, e_ref) * (1.0 / HD) + EPS) * qg_ref[...]
    cos4 = jnp.concatenate([cos] * NPAIR, axis=1)
    sin4 = jnp.concatenate([sin] * NPAIR, axis=1)
    q_ref[...] = ((qn * cos4 + _swap16(qn) * sin4) * (HD ** -0.5)).astype(BF16)

    kn =zk * lax.rsqrt(_head_sum(zk * zk, e_ref[0:256, 0:256]) * (1.0 / HD) + EPS) * kg_ref[...]
    lane = lax.broadcasted_iota(jnp.int32, (TS, 128), 1)
    ck_ref[...] = jnp.where(lane < HD, kn[:, 0:128], kn[:, 128:256])
    cos2 = jnp.concatenate([cos, cos], axis=1)
    sin2 = jnp.concatenate([sin, sin], axis=1)
    kd_ref[...] = (kn * cos2 + _swap16(kn) * sin2).astype(BF16)
    cv_ref[...] = jnp.where(lane < HD, zv[:, 0:128], zv[:, 128:256])
    vd_ref[...] = zv.astype(BF16)


def _pre_cd(x, mod, g, win, lw1, lw2, la1, la2, lg1, lg2, mu_rkv, mu_wag, w0, a0, k_k, qg, kg, e, cos, sin):
    mi = _mod_index(NS_CTX)
    nblk8 = ROWS // 8
    row512 = lambda i: (i, 0)
    half_out = lambda dt: jax.ShapeDtypeStruct((ROWS, HALF), dt)
    out_shape = [half_out(BF16)] * 4 + [half_out(F32)] * 2 + [half_out(BF16)] * 3 + [
                                 jax.ShapeDtypeStruct((ROWS, HALF), BF16),
                                 jax.ShapeDtypeStruct((ROWS, 256), BF16),
                                 jax.ShapeDtypeStruct((ROWS, 256), BF16),
                                 jax.ShapeDtypeStruct((ROWS, 128), F32),
                                 jax.ShapeDtypeStruct((ROWS, 128), F32)]
    out_specs = [pl.BlockSpec((TS, HALF), row512)] * 10 + [pl.BlockSpec((TS, 256), row512)] * 2 + \
                [pl.BlockSpec((TS, 128), row512)] * 2
    return pl.pallas_call(
        _pre_cd_body,
        out_shape=out_shape,
        grid=(NS,),
        in_specs=[pl.BlockSpec((TS, D), row512),
                  pl.BlockSpec((8, D), lambda i: (jnp.maximum(i * (TS // 8) - 1, 0), 0)),
                  pl.BlockSpec((8, D), lambda i: (jnp.minimum((i + 1) * (TS // 8), nblk8 - 1), 0)),
                  pl.BlockSpec((1, 6, D), lambda i: (mi(i, TS), 0, 0)),
                  _const_spec((1, D)), _const_spec(win.shape),
                  _const_spec(lw1.shape), _const_spec(lw2.shape), _const_spec(la1.shape), _const_spec(la2.shape),
                  _const_spec(lg1.shape), _const_spec(lg2.shape),
                  _const_spec((3, HALF)), _const_spec((3, D)), _const_spec((2, HALF)), _const_spec((2, HALF)),
                  _const_spec((1, HALF)), _const_spec((1, HALF)), _const_spec((1, 256)),
                  _const_spec((HALF, HALF)),
                  pl.BlockSpec((TS, 128), lambda i: (i % (L_LAT // TS), 0)),
                  pl.BlockSpec((TS, 128), lambda i: (i % (L_LAT // TS), 0))],
        out_specs=out_specs,
        scratch_shapes=[pltpu.VMEM((TS, D), F32)],
        compiler_params=_params(("arbitrary",)),
        name="pre_cd",
    )(x, x, x, mod, g, win, lw1, lw2, la1, la2, lg1, lg2, mu_rkv, mu_wag, w0, a0, k_k, qg, kg, e, cos, sin)


def _scan_unit(rc, kc, vc, kkc, lwc, ac, ka, get_state, put_state, delay, reverse, masks):
    cum_mask, tri_s, tri_i, eye, lane_lo, diag8, offs = masks
    lw_hi = lwc.astype(BF16)
    res = lwc - lw_hi.astype(F32)
    lw_mid = res.astype(BF16)
    lw_lo = (res - lw_mid.astype(F32)).astype(BF16)
    cum3 = _dot(cum_mask, jnp.concatenate([lw_hi, lw_mid, lw_lo], axis=1))
    cum = cum3[:, 0:128] + cum3[:, 128:256] + cum3[:, 256:384]
    cume = cum - lwc
    if reverse:
        tot, mid = cum[0:1, :], cum[SC // 2:SC // 2 + 1, :]
    else:
        tot, mid = cum[SC - 1:SC, :], cum[SC // 2 - 1:SC // 2, :]
    at = -kkc * jnp.exp(cume - mid)
    rt = rc * jnp.exp(cum - mid)
    e3 = jnp.exp(mid - cum)
    kt = kc * (1.0 + (ac - 1.0) * ka) * e3
    bt = kkc * ac * e3

    def stack(x):
        return jnp.concatenate([jnp.where(lane_lo, x, 0.0), jnp.where(lane_lo, 0.0, x)], axis=0)

    a2 = stack(at)
    r2 = stack(rt)
    v2 = stack(vc).astype(BF16)
    kb = jnp.concatenate([stack(kt), stack(bt)], axis=0).astype(BF16)
    lhs = jnp.concatenate([a2, r2], axis=0).astype(BF16)
    rhs = jnp.concatenate([bt, bt, kt, kt], axis=0).astype(BF16)
    yield
    p = _dot_nt(lhs, rhs)
    a_ab = jnp.where(tri_s, p[0:128, 0:128], 0.0)
    a_ak = jnp.where(tri_s, p[0:128, 128:256], 0.0)
    a_rb = jnp.where(tri_i, p[128:256, 0:128], 0.0)
    a_rk = jnp.where(tri_i, p[128:256, 128:256], 0.0)

    lk = jnp.where(diag8, a_ab, 0.0)
    tinv = eye + lk
    for _ in range(2):
        lkb = lk.astype(BF16)
        yield
        lk = _dot(lkb, lkb)
        yield
        tinv = _dot(tinv.astype(BF16), (eye + lk).astype(BF16))
    for off in offs:
        tb = tinv.astype(BF16)
        yield
        tl = _dot(tb, jnp.where(off, a_ab, 0.0).astype(BF16)).astype(BF16)
        yield
        tinv = tinv + _dot(tl, tb)

    akv = _dot(a_ak.astype(BF16), v2)
    yield
    wu = _dot(tinv.astype(BF16), jnp.concatenate([a2, akv], axis=1).astype(BF16))
    w2 = wu[:, 0:128].astype(BF16)
    u02 = wu[:, 128:256]
    a_ru = jnp.concatenate([a_rk, a_rb], axis=1).astype(BF16)
    r2b = r2.astype(BF16)

    for _ in range(delay):
        yield
    s_prev = get_state()
    sdb = (s_prev * jnp.exp(mid)).astype(BF16)
    yield
    u2 = u02 + _dot_nt(w2, sdb)
    vu = jnp.concatenate([v2, u2.astype(BF16)], axis=0)
    yield
    y2 = _dot(a_ru, vu) + _dot_nt(r2b, sdb)
    s_new = s_prev * jnp.exp(tot) + _dot_tn(vu, kb) * jnp.exp(tot - mid)
    put_state(s_new)
    return y2[0:SC, :] + y2[SC:2 * SC, :]


def _run_lockstep(gens):
    results = [None] * len(gens)
    live = list(range(len(gens)))
    while live:
        still = []
        for u in live:
            try:
                next(gens[u])
                still.append(u)
            except StopIteration as stop:
                results[u] = stop.value
        live = still
    return results


def _scan_body(tab_ref, *refs):
    n_in = NMEM * 2 * 6
    in_refs = refs[:n_in]
    ka_ref = refs[n_in]
    s0_refs = refs[n_in + 1:n_in + 1 + 2 * NMEM]
    yf_ref, yb_ref, sf_ref, sb_ref, st_ref = refs[n_in + 1 + 2 * NMEM:]
    t = pl.program_id(0)
    first = tab_ref[_T_FIRST, t]
    last = tab_ref[_T_LAST, t]
    is_lat = tab_ref[_T_LAT, t]

    def unit(m, d, p):
        return (m * 2 + d) * NPAIR + p

    @pl.when(first == 1)
    def _():
        zero = jnp.zeros((HD, HD), F32)
        lat_f = is_lat.astype(F32)
        for m in range(NMEM):
            for d in range(2):
                for p in range(NPAIR):
                    lo = s0_refs[m * 2 + d][0, 2 * p] * lat_f
                    hi = s0_refs[m * 2 + d][0, 2 * p + 1] * lat_f
                    st_ref[unit(m, d, p)] = jnp.concatenate(
                        [jnp.concatenate([lo, zero], axis=1), jnp.concatenate([zero, hi], axis=1)], axis=0)

    ri = lax.broadcasted_iota(jnp.int32, (SC, SC), 0)
    ci = lax.broadcasted_iota(jnp.int32, (SC, SC), 1)
    r2 = lax.broadcasted_iota(jnp.int32, (128, 128), 0)
    c2 = lax.broadcasted_iota(jnp.int32, (128, 128), 1)
    same = (r2 & SC) == (c2 & SC)
    eye = (r2 == c2).astype(F32)
    diag8 = (r2 >> 3) == (c2 >> 3)
    lane_lo = lax.broadcasted_iota(jnp.int32, (SC, 128), 1) < HD

    dir_masks = []
    for reverse in (False, True):
        early, late = (r2, c2) if reverse else (c2, r2)
        offs = tuple(((r2 >> sh + 1) == (c2 >> sh + 1)) & ((late & (1 << sh)) != 0) & ((early & (1 << sh)) == 0)
                     for sh in (3, 4, 5))
        if reverse:
            dir_masks.append(((ri <= ci).astype(BF16), same & (r2 < c2), same & (r2 <= c2), eye, lane_lo, diag8, offs))
        else:
            dir_masks.append(((ri >= ci).astype(BF16), same & (r2 > c2), same & (r2 >= c2), eye, lane_lo, diag8, offs))

    state = {}
    gens, dests = [], []
    for sub in range(SUB):
        for d, reverse in enumerate((False, True)):
            y_ref = yb_ref if reverse else yf_ref
            rows = pl.ds((SUB - 1 - sub if reverse else sub) * SC, SC)
            for m in range(NMEM):
                unit_refs = in_refs[(m * 2 + d) * 6:(m * 2 + d + 1) * 6]
                for p in range(NPAIR):
                    u = unit(m, d, p)
                    ls = slice(p * 128, (p + 1) * 128)
                    rc, kc, vc, kkc, lwc, ac = (ref[rows, ls].astype(F32) for ref in unit_refs)
                    get_state = functools.partial(state.get, u) if sub else functools.partial(lambda u: st_ref[u], u)
                    gens.append(_scan_unit(rc, kc, vc, kkc, lwc, ac, ka_ref[:, ls], get_state,
                                           functools.partial(state.__setitem__, u), 2 * sub, reverse, dir_masks[d]))
                    dests.append((y_ref, m, rows, ls))
    for (y_ref, m, rows, ls), y in zip(dests, _run_lockstep(gens)):
        y_ref[m, 0, rows, ls] = y.astype(BF16)
    for u, s_new in state.items():
        st_ref[u] = s_new

    @pl.when(last == 1)
    def _():
        for m in range(NMEM):
            for d, s_out in enumerate((sf_ref, sb_ref)):
                for p in range(NPAIR):
                    s = st_ref[unit(m, d, p)]
                    s_out[m, 0, 2 * p] = s[0:HD, 0:HD]
                    s_out[m, 0, 2 * p + 1] = s[HD:2 * HD, HD:2 * HD]


_T_FWD, _T_BWD = 0, NMEM
_T_YF, _T_YB = 2 * NMEM, 2 * NMEM + 1
_T_FIRST, _T_LAST, _T_LAT = 2 * NMEM + 2, 2 * NMEM + 3, 2 * NMEM + 4
_T_SIN = 2 * NMEM + 5
_T_SOUT = 3 * NMEM + 5
_Y_SLOTS_CTX = ROWS_CTX // SCB // NMEM


def _scan_tables():
    cols = []
    for group, (nb, ln, base, slot0) in enumerate(((N_CTX, L_CTX, 0, 0), (N_LAT, L_LAT, ROWS_CTX, _Y_SLOTS_CTX))):
        nc = ln // SCB
        for p in range(nb // NMEM):
            batches = [p + m * (nb // NMEM) for m in range(NMEM)]
            blks = [(base + b * ln) // SCB for b in batches]
            for c in range(nc):
                col = [blk + c for blk in blks] + [blk + nc - 1 - c for blk in blks]
                col += [slot0 + p * nc + c, slot0 + p * nc + nc - 1 - c]
                col += [int(c == 0), int(c == nc - 1), group]
                col += [b if group == 1 else 0 for b in batches]
                col += [p if group == 0 else N_CTX // NMEM]
                cols.append(col)
    return np.asarray(cols, np.int32).T


def _scan(r, k, v, kk, lwf, lwb, af, ab, k_a, s0f, s0b):
    tab = _scan_tables()
    steps = tab.shape[1]

    def row_spec(table_row):
        return pl.BlockSpec((SCB, HALF), lambda t, tab: (tab[table_row, t], 0))

    def state_spec(table_row):
        return pl.BlockSpec((1, 2 * NPAIR, HD, HD), lambda t, tab: (tab[table_row, t], 0, 0, 0))

    in_specs, args = [], []
    for m in range(NMEM):
        for table_row, lw, a in ((_T_FWD + m, lwf, af), (_T_BWD + m, lwb, ab)):
            in_specs += [row_spec(table_row)] * 6
            args += [r, k, v, kk, lw, a]
    in_specs.append(pl.BlockSpec((1, HALF), lambda t, tab: (0, 0)))
    args.append(k_a)
    for m in range(NMEM):
        in_specs += [state_spec(_T_SIN + m)] * 2
        args += [s0f, s0b]
    y_shape = jax.ShapeDtypeStruct((NMEM, ROWS // SCB // NMEM, SCB, HALF), BF16)
    s_shape = jax.ShapeDtypeStruct((NMEM, N_CTX // NMEM + 1, 2 * NPAIR, HD, HD), F32)
    s_out = pl.BlockSpec((NMEM, 1, 2 * NPAIR, HD, HD), lambda t, tab: (0, tab[_T_SOUT, t], 0, 0, 0))
    grid_spec = pltpu.PrefetchScalarGridSpec(
        num_scalar_prefetch=1,
        grid=(steps,),
        in_specs=in_specs,
        out_specs=[pl.BlockSpec((NMEM, 1, SCB, HALF), lambda t, tab: (0, tab[_T_YF, t], 0, 0)),
                   pl.BlockSpec((NMEM, 1, SCB, HALF), lambda t, tab: (0, tab[_T_YB, t], 0, 0)),
                   s_out, s_out],
        scratch_shapes=[pltpu.VMEM((NMEM * 2 * NPAIR, 128, 128), F32)],
    )
    return pl.pallas_call(
        _scan_body,
        out_shape=[y_shape, y_shape, s_shape, s_shape],
        grid_spec=grid_spec,
        compiler_params=_params(("arbitrary",)),
        name="wkv_scan",
    )(jnp.asarray(tab), *args)


def _attn_body(q_ref, k_ref, v_ref, *rest, cached):
    if cached:
        ck_ref, cv_ref, o_ref = rest
    else:
        (o_ref,) = rest
    lane_lo = lax.broadcasted_iota(jnp.int32, (QB, 128), 1) < HD

    def head(pr, half):
        qp = q_ref[:, pr * 128:(pr + 1) * 128]
        kvh = (2 * pr) // 4
        ks = slice(kvh * 128, (kvh + 1) * 128)
        qh = jnp.where(lane_lo if half == 0 else jnp.logical_not(lane_lo), qp, jnp.zeros_like(qp))
        s_new = _dot_nt(qh, k_ref[:, ks])
        if cached:
            s_old = _dot_nt(qh, ck_ref[0, :, ks])
            yield
            m = jnp.maximum(jnp.max(s_new, -1, keepdims=True), jnp.max(s_old, -1, keepdims=True))
            p_new = jnp.exp(s_new - m)
            p_old = jnp.exp(s_old - m)
            den = jnp.sum(p_new, -1, keepdims=True) + jnp.sum(p_old, -1, keepdims=True)
            yield
            o = _dot(p_new.astype(BF16), v_ref[:, ks]) + _dot(p_old.astype(BF16), cv_ref[0, :, ks])
        else:
            yield
            m = jnp.max(s_new, -1, keepdims=True)
            p_new = jnp.exp(s_new - m)
            den = jnp.sum(p_new, -1, keepdims=True)
            yield
            o = _dot(p_new.astype(BF16), v_ref[:, ks])
        return o / den

    outs = _run_lockstep([head(pr, half) for pr in range(NPAIR) for half in range(2)])
    for pr in range(NPAIR):
        o_ref[:, pr * 128:(pr + 1) * 128] = jnp.where(lane_lo, outs[2 * pr], outs[2 * pr + 1]).astype(BF16)


def _attention(q, kd, vd, base, nb, ln, cache=None):
    nq = ln // QB
    blk0 = base // QB
    seq0 = base // ln
    in_specs = [pl.BlockSpec((QB, HALF), lambda b, j: (blk0 + b * nq + j, 0)),
                pl.BlockSpec((ln, 256), lambda b, j: (seq0 + b, 0)),
                pl.BlockSpec((ln, 256), lambda b, j: (seq0 + b, 0))]
    args = [q, kd, vd]
    if cache is not None:
        in_specs += [pl.BlockSpec((1, PAST, 256), lambda b, j: (b, 0, 0))] * 2
        args += list(cache)
    return pl.pallas_call(
        functools.partial(_attn_body, cached=cache is not None),
        out_shape=jax.ShapeDtypeStruct((nb * ln, HALF), BF16),
        grid=(nb, nq),
        in_specs=in_specs,
        out_specs=pl.BlockSpec((QB, HALF), lambda b, j: (b * nq + j, 0)),
        compiler_params=_params(("arbitrary", "arbitrary")),
        name="gqa_cached" if cache is not None else "gqa",
    )(*args)


def _post_cd_body(x_ref, mod_ref, yf_ref, yb_ref, r_ref, k_ref, v_ref, af_ref, ab_ref, gg_ref,
                  atc_ref, atl_ref, wout_ref, e_ref, lng_ref, lnb_ref, rk_ref, ka_ref, o_ref):
    i = pl.program_id(0)
    gate = mod_ref[0, 2:3, :]
    y = yf_ref[0].reshape(TS, HALF).astype(F32) + yb_ref[0].reshape(TS, HALF).astype(F32)
    yc = y - _head_sum(y, e_ref) * (1.0 / HD)
    yn = yc * lax.rsqrt(_head_sum(yc * yc, e_ref) * (1.0 / HD) + GN_EPS) * lng_ref[...] + lnb_ref[...]
    f32 = lambda ref: ref[...].astype(F32)
    kd2 = f32(k_ref) * (2.0 + (f32(af_ref) + f32(ab_ref) - 2.0) * ka_ref[...])
    bonus = _head_sum(f32(r_ref) * kd2 * rk_ref[...], e_ref) * f32(v_ref)
    yc_out = ((yn + bonus) * f32(gg_ref)).astype(BF16)
    att = jnp.where(i >= NS_CTX, atl_ref[...], atc_ref[...])
    out = _dot(yc_out, wout_ref[0:HALF, :]) + _dot(att, wout_ref[HALF:2 * HALF, :])
    o_ref[...] = x_ref[...] + gate * out


def _post_cd(x, mod, yf, yb, r, k, v, af, ab, gg, att_c, att_l, wout, e, lng, lnb, r_k, k_a):
    mi = _mod_index(NS_CTX)
    row = lambda i: (i, 0)
    half = pl.BlockSpec((TS, HALF), row)
    ctx_per_mem = NS_CTX // NMEM
    lat_per_mem = (NS - NS_CTX) // NMEM

    def y_index(i):
        j = i - NS_CTX
        return (jnp.where(i < NS_CTX, i // ctx_per_mem, j // lat_per_mem),
                jnp.where(i < NS_CTX, i % ctx_per_mem, ctx_per_mem + j % lat_per_mem), 0, 0)

    y_spec = pl.BlockSpec((1, TS // SCB, SCB, HALF), y_index)
    return pl.pallas_call(
        _post_cd_body,
        out_shape=jax.ShapeDtypeStruct((ROWS, D), F32),
        grid=(NS,),
        in_specs=[pl.BlockSpec((TS, D), row),
                  pl.BlockSpec((1, 6, D), lambda i: (mi(i, TS), 0, 0)), y_spec, y_spec] + [half] * 6 +
                 [pl.BlockSpec((TS, HALF), lambda i: (jnp.minimum(i, NS_CTX - 1), 0)),
                  pl.BlockSpec((TS, HALF), lambda i: (jnp.maximum(i - NS_CTX, 0), 0)),
                  _layer_spec((D, D), 1), _const_spec((HALF, HALF)),
                  _const_spec((1, HALF)), _const_spec((1, HALF)), _const_spec((1, HALF)), _const_spec((1, HALF))],
        out_specs=pl.BlockSpec((TS, D), row),
        compiler_params=_params(("arbitrary",)),
        name="post_cd",
    )(x, mod, yf, yb, r, k, v, af, ab, gg, att_c, att_l, wout, e, lng, lnb, r_k, k_a)


def _rope_tables():
    half = 16
    inv = 10000.0 ** (-jnp.arange(half, dtype=F32) / half)
    t = jnp.arange(L_LAT)
    cos_parts, sin_parts = [], []
    for pos in (t // GRID_W, t % GRID_W):
        ang = pos.astype(F32)[:, None] * inv[None, :]
        cos_parts += [jnp.cos(ang), jnp.cos(ang)]
        sin_parts += [-jnp.sin(ang), jnp.sin(ang)]
    cos = jnp.concatenate(cos_parts * 2, axis=1)
    sin = jnp.concatenate(sin_parts * 2, axis=1)
    return cos, sin


def kernel(x_prompt, x_sample, cache_k, cache_v, state_wkv_fwd, state_wkv_bwd, c, c_ctx, mod_w, mod_b, norm1_g, norm2_g, w_out, ffn_up, ffn_conv_w, ffn_conv_b, ffn_down, final_g, ab_w_in, a_vnorm_g, a_vnorm_b, a_ws, a_bs, b_conv_w, b_conv_b, b_norm_g, cd_w_in, c_mu_rkv, c_mu_wag, c_w0, c_w1, c_w2, c_a0, c_a1, c_a2, c_g1, c_g2, c_k_k, c_k_a, c_r_k, c_ln_g, c_ln_b, d_q_g, d_k_g):
    row = lambda a: a.reshape(1, -1)
    cvec =jnp.concatenate([c_ctx[None, :], c, jnp.zeros((16 - 1 - N_LAT, D), F32)], axis=0)
    mods = _modulation(cvec, mod_w, mod_b).reshape(2, 16, 6, D)

    bs = jnp.broadcast_to(a_bs[0][:, :, None], (A_GROUPS, CHUNK, 128))
    cw = jnp.concatenate([b_conv_w[0], jnp.zeros((1, HALF), F32)], axis=0)
    w_out_b, ffn_up_b, ffn_down_b = w_out.astype(BF16), ffn_up.astype(BF16), ffn_down.astype(BF16)
    x = _mix_ab(x_prompt.reshape(ROWS_CTX, D), x_sample.reshape(ROWS_LAT, D), mods[0], row(norm1_g[0]),
                ab_w_in[0].astype(BF16), w_out_b,
                a_ws[0].astype(BF16), bs, row(a_vnorm_g[0]), row(a_vnorm_b[0]),
                cw, row(b_conv_b[0]), row(b_norm_g[0]))
    x = _ffn(x, 0, NT, 0, mods[0], row(norm2_g[0]), ffn_up_b, ffn_conv_w, row(ffn_conv_b[0]),
             ffn_down_b, row(final_g), final=False)

    w = cd_w_in[0]
    wk = w[:, 4 * HALF:4 * HALF + 128]
    wv = w[:, 4 * HALF + 128:4 * HALF + 256]
    win = jnp.concatenate([w[:, :4 * HALF], wk[:, :HD], wk[:, :HD], wk[:, HD:], wk[:, HD:],
                           wv[:, :HD], wv[:, :HD], wv[:, HD:], wv[:, HD:]], axis=1).astype(BF16)
    lw1 = jnp.concatenate([c_w1[0, 0], c_w1[0, 1]], axis=1).astype(BF16)
    la1 = jnp.concatenate([c_a1[0, 0], c_a1[0, 1]], axis=1).astype(BF16)
    ones_blk = jnp.asarray(np.kron(np.eye(HALF // HD), np.ones((HD, HD))), BF16)
    cos, sin = _rope_tables()
    k_a = row(c_k_a[0])
    (r, k, v, kk, lwf, lwb, af, ab, gg, q, kd, vd, ck, cv) = _pre_cd(
        x, mods[1], row(norm1_g[1]), win, lw1, c_w2[0].astype(BF16), la1, c_a2[0].astype(BF16),
        c_g1[0].astype(BF16), c_g2[0].astype(BF16), c_mu_rkv[0], c_mu_wag[0], c_w0[0], c_a0[0],
        row(c_k_k[0]), jnp.tile(d_q_g[0], HALF // HD)[None, :], jnp.tile(d_k_g[0], 4)[None, :],
        ones_blk, cos, sin)

    yf, yb, s_f, s_b = _scan(r, k, v, kk, lwf, lwb, af, ab, k_a, state_wkv_fwd[:, 0], state_wkv_bwd[:, 0])

    def dup_cache(t):
        t = t[:, 0].astype(BF16)
        return jnp.concatenate([t[:, :, 0], t[:, :, 0], t[:, :, 1], t[:, :, 1]], axis=-1)

    att_c = _attention(q, kd, vd, 0, N_CTX, L_CTX)
    att_l = _attention(q, kd, vd, ROWS_CTX, N_LAT, L_LAT, cache=(dup_cache(cache_k), dup_cache(cache_v)))

    x = _post_cd(x, mods[1], yf, yb, r, k, v, af, ab, gg, att_c, att_l, w_out_b, ones_blk,
                 row(c_ln_g[0]), row(c_ln_b[0]), row(c_r_k[0]), k_a)
    ffn1 = (1, mods[1], row(norm2_g[1]), ffn_up_b, ffn_conv_w, row(ffn_conv_b[1]), ffn_down_b, row(final_g))
    y_prompt = _ffn(x, 0, NT_CTX, *ffn1, final=True).reshape(N_CTX, L_CTX, D)
    y_sample = _ffn(x, NT_CTX, NT - NT_CTX, *ffn1, final=True).reshape(N_LAT, L_LAT, D)
    new_k = ck[:ROWS_CTX].reshape(N_CTX, 1, L_CTX, 2, HD)
    new_v = cv[:ROWS_CTX].reshape(N_CTX, 1, L_CTX, 2, HD)
    return (y_prompt, y_sample, new_k, new_v,
            s_f[:, :N_CTX // NMEM].reshape(N_CTX, 1, 2 * NPAIR, HD, HD),
            s_b[:, :N_CTX // NMEM].reshape(N_CTX, 1, 2 * NPAIR, HD, HD))
```

```python
import functools
import math

import jax
import jax.numpy as jnp
import numpy as np
from jax import lax
from jax.experimental import pallas as pl
from jax.experimental.pallas import tpu as pltpu

F32 = jnp.float32
BF16 = jnp.bfloat16

D = 1024
N_CTX, L_CTX = 16, 256
N_LAT, L_LAT = 8, 1024
ROWS_CTX = N_CTX * L_CTX
ROWS_LAT = N_LAT * L_LAT
ROWS = ROWS_CTX + ROWS_LAT
PAST = 256
GRID_W = 64
EPS = 1e-6
GN_EPS = 64e-5
HALF = 512
HD = 64
NPAIR = HALF // 128
D_FF = 2816
B_CONV = 31
CHUNK = 128
A_GROUPS = 4

TM = 1024
NT = ROWS // TM
NT_CTX = ROWS_CTX // TM
SEG = 256
NSEG = TM // SEG
HALO = 16
TS = 512
NS = ROWS // TS
NS_CTX = ROWS_CTX // TS
FB = 256
NJ = D_FF // FB
SC = 64
SUB = 2
SCB = SC * SUB
NMEM = 2
QB = 256
VMEM_LIMIT = 56 * 1024 * 1024

_NT_DIMS = (((1,), (1,)), ((), ()))
_TN_DIMS = (((0,), (0,)), ((), ()))


def _dot(a, b):
    return jnp.dot(a, b, preferred_element_type=F32)


def _dot_nt(a, b):
    return lax.dot_general(a, b, _NT_DIMS, preferred_element_type=F32)


def _dot_tn(a, b):
    return lax.dot_general(a, b, _TN_DIMS, preferred_element_type=F32)


def _rms(x, eps=EPS):
    return x * lax.rsqrt(jnp.mean(x * x, -1, keepdims=True) + eps)


def _sigmoid(x):
    return 1.0 / (1.0 + jnp.exp(-x))


def _silu(x):
    return x * _sigmoid(x)


def _gelu_tanh(x):
    return 0.5 * x * (1.0 + jnp.tanh(math.sqrt(2.0 / math.pi) * (x + 0.044715 * (x * x * x))))


def _const_spec(shape):
    return pl.BlockSpec(shape, lambda *_: (0,) * len(shape), pipeline_mode=pl.Buffered(1))


def _layer_spec(shape, layer):
    return pl.BlockSpec((None,) + tuple(shape), lambda *_: (layer,) + (0,) * len(shape), pipeline_mode=pl.Buffered(1))


def _params(sem):
    return pltpu.CompilerParams(dimension_semantics=sem, vmem_limit_bytes=VMEM_LIMIT)


def _mod_body(c_ref, w_ref, b_ref, o_ref):
    s = _silu(c_ref[...])
    o_ref[0] = jnp.dot(s, w_ref[0], preferred_element_type=F32,
                       precision=lax.Precision.HIGHEST) + b_ref[0]


def _modulation(cvec, mod_w, mod_b):
    depth = mod_w.shape[0]
    nb = 1536
    return pl.pallas_call(
        _mod_body,
        out_shape=jax.ShapeDtypeStruct((depth, 16, 6 * D), F32),
        grid=(depth, 6 * D // nb),
        in_specs=[pl.BlockSpec((16, D), lambda l, j: (0, 0)),
                  pl.BlockSpec((1, D, nb), lambda l, j: (l, 0, j)),
                  pl.BlockSpec((1, 1, nb), lambda l, j: (l, 0, j))],
        out_specs=pl.BlockSpec((1, 16, nb), lambda l, j: (l, 0, j)),
        compiler_params=_params(("arbitrary", "arbitrary")),
        name="modulation",
    )(cvec, mod_w, mod_b.reshape(depth, 1, 6 * D))


def _mod_index(tiles_ctx):
    def idx(i, tile_rows):
        return jnp.where(i >= tiles_ctx, (i - tiles_ctx) // (L_LAT // tile_rows) + 1, 0)
    return idx


def _mix_ab_body(xc_ref, xl_ref, mod_ref, g_ref, win_ref, wout_ref, ws_ref, bs_ref, vg_ref, vb_ref,
                 cw_ref, cb_ref, ng_ref, o_ref, cat_ref, pad_ref, shift_ref):
    i = pl.program_id(0)
    is_lat = i >= NT_CTX
    lat = is_lat.astype(F32)

    def x_rows(r0):
        return jnp.where(is_lat, xl_ref[pl.ds(r0, SEG), :], xc_ref[pl.ds(r0, SEG), :])

    shift = mod_ref[0, 0:1, :]
    scale = mod_ref[0, 1:2, :]
    gate = mod_ref[0, 2:3, :]

    def phase1(s, carry):
        r0 = s * SEG
        x = x_rows(r0)
        h = (_rms(x) * g_ref[...] * (1.0 + scale) + shift).astype(BF16)
        z = _dot(h, win_ref[...])
        za = _gelu_tanh(z[:, :2 * HALF])
        u = za[:, :HALF]
        v = za[:, HALF:]
        vc = v - jnp.mean(v, -1, keepdims=True)
        vn = vc * lax.rsqrt(jnp.mean(vc * vc, -1, keepdims=True) + EPS) * vg_ref[...] + vb_ref[...]
        vnb = vn.astype(BF16)
        for n in range(SEG // CHUNK):
            parts = []
            for g in range(A_GROUPS):
                vv = vnb[n * CHUNK:(n + 1) * CHUNK, g * 128:(g + 1) * 128]
                parts.append(_dot(ws_ref[g], vv) + bs_ref[g])
            vp = jnp.concatenate(parts, axis=1)
            ya = u[n * CHUNK:(n + 1) * CHUNK, :] * vp
            cat_ref[pl.ds(r0 + n * CHUNK, CHUNK), 0:HALF] = ya.astype(BF16)
        zb = z[:, 2 * HALF:]
        pad_ref[s, HALO:HALO + SEG, :] = zb[:, :HALF] * _sigmoid(zb[:, HALF:])
        return carry

    def halos(s):
        zeros_h = jnp.zeros((HALO, HALF), F32)
        left = pad_ref[s - 1, SEG:SEG + HALO, :] * lat if s > 0 else zeros_h
        right = pad_ref[s + 1, HALO:2 * HALO, :] * lat if s < NSEG - 1 else zeros_h
        pad_ref[s, 0:HALO, :] = left
        pad_ref[s, HALO + SEG:2 * HALO + SEG, :] = right

    RB = 32

    shift_rows = SEG + 2 * HALO - 8

    def phase3(s, carry):
        r0 = s * SEG
        for sh in range(1, 8):
            shift_ref[sh - 1] = pad_ref[s, sh:sh + shift_rows, :]
        for rb in range(SEG // RB):
            acc = jnp.zeros((RB, HALF), F32) + cb_ref[...]
            for k in range(B_CONV):
                q8, sh = divmod(HALO - B_CONV // 2 + k, 8)
                off = 8 * q8 + rb * RB
                win = pad_ref[s, off:off + RB, :] if sh == 0 else shift_ref[sh - 1, off:off + RB, :]
                acc = acc + cw_ref[k:k + 1, :] * win
            yb = _silu(_rms(acc) * ng_ref[...])
            cat_ref[pl.ds(r0 + rb * RB, RB), HALF:2 * HALF] = yb.astype(BF16)
        out = _dot(cat_ref[pl.ds(r0, SEG), :], wout_ref[...])
        o_ref[pl.ds(r0, SEG), :] = x_rows(r0) + gate * out
        return carry

    phase1(0, 0)
    for s in range(NSEG):
        if s + 1 < NSEG:
            phase1(s + 1, 0)
        halos(s)
        phase3(s, 0)


def _mix_ab(xc, xl, mod, g, win, wout, ws, bs, vg, vb, cw, cb, ng):
    mi = _mod_index(NT_CTX)
    return pl.pallas_call(
        _mix_ab_body,
        out_shape=jax.ShapeDtypeStruct((ROWS, D), F32),
        grid=(NT,),
        in_specs=[pl.BlockSpec((TM, D), lambda i: (jnp.minimum(i, NT_CTX - 1), 0)),
                  pl.BlockSpec((TM, D), lambda i: (jnp.maximum(i - NT_CTX, 0), 0)),
                  pl.BlockSpec((1, 6, D), lambda i: (mi(i, TM), 0, 0)),
                  _const_spec((1, D)), _const_spec((D, 4 * HALF)), _layer_spec((D, D), 0),
                  _const_spec((A_GROUPS, CHUNK, CHUNK)), _const_spec((A_GROUPS, CHUNK, 128)),
                  _const_spec((1, HALF)), _const_spec((1, HALF)),
                  _const_spec((B_CONV + 1, HALF)), _const_spec((1, HALF)), _const_spec((1, HALF))],
        out_specs=pl.BlockSpec((TM, D), lambda i: (i, 0)),
        scratch_shapes=[pltpu.VMEM((TM, D), BF16),
                        pltpu.VMEM((NSEG, SEG + 2 * HALO, HALF), F32),
                        pltpu.VMEM((7, SEG + 2 * HALO - 8, HALF), F32)],
        compiler_params=_params(("arbitrary",)),
        name="mix_ab",
    )(xc, xl, mod, g, win, wout, ws, bs, vg, vb, cw, cb, ng)


def _ffn_body(x_ref, mod_ref, g_ref, up_ref, cw_ref, cb_ref, down_ref, fg_ref, o_ref,
              h_ref, gated_ref, cza_ref, czb_ref, *, tile0, final):
    is_ctx = tile0 + pl.program_id(0) < NT_CTX
    shift = mod_ref[0, 3:4, :]
    scale = mod_ref[0, 4:5, :]
    h_ref[...] = (_rms(x_ref[...]) * g_ref[...] * (1.0 + scale) + shift).astype(BF16)

    def up_proj(col):
        return _dot(h_ref[...], up_ref[:, col * FB:(col + 1) * FB])

    def conv_branch(col, z, cz_ref):
        cs = slice(col * FB, (col + 1) * FB)
        w0, w1, w2, b = cw_ref[0:1, cs], cw_ref[1:2, cs], cw_ref[2:3, cs], cb_ref[:, cs]
        cz_ref[...] = w0 * pltpu.roll(z, 1, 0) + w1 * z + w2 * pltpu.roll(z, TM - 1, 0) + b
        for t0 in range(0, TM, L_CTX):
            t1 = t0 + L_CTX - 1
            first = w1 * z[t0:t0 + 1, :] + w2 * z[t0 + 1:t0 + 2, :] + b
            last = w0 * z[t1 - 1:t1, :] + w1 * z[t1:t1 + 1, :] + b
            cz_ref[t0:t0 + 1, :] = first if t0 == 0 else jnp.where(is_ctx, first, cz_ref[t0:t0 + 1, :])
            cz_ref[t1:t1 + 1, :] = last if t1 == TM - 1 else jnp.where(is_ctx, last, cz_ref[t1:t1 + 1, :])

    for j in range(NJ):
        conv_branch(j, up_proj(j), cza_ref)
        conv_branch(NJ + j, up_proj(NJ + j), czb_ref)
        gated_ref[:, j * FB:(j + 1) * FB] = (_silu(czb_ref[...]) * cza_ref[...]).astype(BF16)

    y = x_ref[...] + mod_ref[0, 5:6, :] * _dot(gated_ref[...], down_ref[...])
    if final:
        y = _rms(y) * fg_ref[...]
    o_ref[...] = y


def _ffn(x, tile0, ntiles, layer, mod, g, up, cw, cb, down, fg, final):
    mi = _mod_index(NT_CTX)
    return pl.pallas_call(
        functools.partial(_ffn_body, tile0=tile0, final=final),
        out_shape=jax.ShapeDtypeStruct((ntiles * TM, D), F32),
        grid=(ntiles,),
        in_specs=[pl.BlockSpec((TM, D), lambda i: (tile0 + i, 0)),
                  pl.BlockSpec((1, 6, D), lambda i: (mi(tile0 + i, TM), 0, 0)),
                  _const_spec((1, D)),
                  _layer_spec((D, 2 * D_FF), layer), _layer_spec((3, 2 * D_FF), layer), _const_spec((1, 2 * D_FF)),
                  _layer_spec((D_FF, D), layer), _const_spec((1, D))],
        out_specs=pl.BlockSpec((TM, D), lambda i: (i, 0)),
        scratch_shapes=[pltpu.VMEM((TM, D), BF16), pltpu.VMEM((TM, D_FF), BF16),
                        pltpu.VMEM((TM, FB), F32), pltpu.VMEM((TM, FB), F32)],
        compiler_params=_params(("arbitrary",)),
        name="conv_ffn",
    )(x, mod, g, up, cw, cb, down, fg)


def _head_sum(x, e_ref):
    return _dot(x.astype(BF16), e_ref[...])


def _swap16(x):
    lane = lax.broadcasted_iota(jnp.int32, x.shape, 1)
    n = x.shape[1]
    return jnp.where((lane & 16) == 0, pltpu.roll(x, n - 16, 1), pltpu.roll(x, 16, 1))


def _pre_cd_body(x_ref, xp_ref, xn_ref, mod_ref, g_ref, win_ref, lw1_ref, lw2_ref, la1_ref, la2_ref,
                 lg1_ref, lg2_ref, mu_rkv_ref, mu_wag_ref, w0_ref, a0_ref, kk_ref, qg_ref, kg_ref,
                 e_ref, cos_ref, sin_ref,
                 r_ref, k_ref, v_ref, kkn_ref, lwf_ref, lwb_ref, af_ref, ab_ref, gg_ref,
                 q_ref, kd_ref, vd_ref, ck_ref, cv_ref, sh_ref):
    i = pl.program_id(0)
    is_lat = i >= NS_CTX
    shift = mod_ref[0, 0:1, :]
    scale = mod_ref[0, 1:2, :]
    gn = g_ref[...]

    def adaln(x):
        return _rms(x) * gn * (1.0 + scale) + shift

    h = adaln(x_ref[...])
    hh = adaln(jnp.concatenate([xp_ref[...], xn_ref[...]], axis=0))
    hb = h.astype(BF16)

    seq = jnp.where(is_lat, L_LAT, L_CTX)
    has_prev = (((i * TS) & (seq - 1)) != 0).astype(F32)
    has_next = ((((i + 1) * TS) & (seq - 1)) != 0).astype(F32)

    def token_shift(z, zh, mix):
        n = z.shape[1]
        dst = sh_ref.at[:, 0:n]
        dst[...] = mix(z, 0.5 * (pltpu.roll(z, 1, 0) + pltpu.roll(z, TS - 1, 0)) - z)

        def fixed(t, nb):
            zt = z[t:t + 1, :]
            return mix(zt, 0.5 * nb - zt)

        dst[0:1, :] = fixed(0, zh[7:8, :] * has_prev + z[1:2, :])
        dst[TS - 1:TS, :] = fixed(TS - 1, z[TS - 2:TS - 1, :] + zh[8:9, :] * has_next)
        for t in range(L_CTX, TS, L_CTX):
            dst[t - 1:t, :] = jnp.where(is_lat, dst[t - 1:t, :], fixed(t - 1, z[t - 2:t - 1, :]))
            dst[t:t + 1, :] = jnp.where(is_lat, dst[t:t + 1, :], fixed(t, z[t + 1:t + 2, :]))
        return dst[...]

    z = _dot(hb, win_ref[:, 0:3 * HALF])
    zq = _dot(hb, win_ref[:, 3 * HALF:4 * HALF])
    zk = _dot(hb, win_ref[:, 4 * HALF:4 * HALF + 256])
    zv = _dot(hb, win_ref[:, 4 * HALF + 256:4 * HALF + 512])
    zh = _dot(hh.astype(BF16), win_ref[:, 0:3 * HALF])
    outs = []
    for c in range(3):
        mu = mu_rkv_ref[c:c + 1, :]
        outs.append(token_shift(z[:, c * HALF:(c + 1) * HALF], zh[:, c * HALF:(c + 1) * HALF],
                                lambda zc, dz: zc + dz * mu))
    r, k, v = outs
    r_ref[...] = r.astype(BF16)
    k_ref[...] = k.astype(BF16)
    v_ref[...] = v.astype(BF16)
    kk = k * kk_ref[...]
    kkn_ref[...] = (kk * lax.rsqrt(_head_sum(kk * kk, e_ref) + 1e-12)).astype(BF16)

    dh = token_shift(h, hh, lambda zc, dz: dz)
    xw = (h + dh * mu_wag_ref[0:1, :]).astype(BF16)
    xa = (h + dh * mu_wag_ref[1:2, :]).astype(BF16)
    xg = (h + dh * mu_wag_ref[2:3, :]).astype(BF16)
    tw = jnp.tanh(_dot(xw, lw1_ref[...])).astype(BF16)
    ta = _dot(xa, la1_ref[...]).astype(BF16)
    for d, (lw_out, a_out) in enumerate(((lwf_ref, af_ref), (lwb_ref, ab_ref))):
        wl = w0_ref[d:d + 1, :] + _dot(tw[:, d * 64:(d + 1) * 64], lw2_ref[d])
        lw_out[...] = (-math.exp(-0.5)) * _sigmoid(wl)
        a_out[...] = _sigmoid(a0_ref[d:d + 1, :] + _dot(ta[:, d * 64:(d + 1) * 64], la2_ref[d])).astype(BF16)
    gg_ref[...] = _dot(_sigmoid(_dot(xg, lg1_ref[...])).astype(BF16), lg2_ref[...]).astype(BF16)

    lat_f = is_lat.astype(F32)
    cos = cos_ref[...] * lat_f + (1.0 - lat_f)
    sin = sin_ref[...] * lat_f
    qn = zq * lax.rsqrt(_head_sum(zq * zq, e_ref) * (1.0 / HD) + EPS) * qg_ref[...]
    cos4 = jnp.concatenate([cos] * NPAIR, axis=1)
    sin4 = jnp.concatenate([sin] * NPAIR, axis=1)
    q_ref[...] = ((qn * cos4 + _swap16(qn) * sin4) * (HD ** -0.5)).astype(BF16)

    kn =zk * lax.rsqrt(_head_sum(zk * zk, e_ref[0:256, 0:256]) * (1.0 / HD) + EPS) * kg_ref[...]
    lane = lax.broadcasted_iota(jnp.int32, (TS, 128), 1)
    ck_ref[...] = jnp.where(lane < HD, kn[:, 0:128], kn[:, 128:256])
    cos2 = jnp.concatenate([cos, cos], axis=1)
    sin2 = jnp.concatenate([sin, sin], axis=1)
    kd_ref[...] = (kn * cos2 + _swap16(kn) * sin2).astype(BF16)
    cv_ref[...] = jnp.where(lane < HD, zv[:, 0:128], zv[:, 128:256])
    vd_ref[...] = zv.astype(BF16)


def _pre_cd(x, mod, g, win, lw1, lw2, la1, la2, lg1, lg2, mu_rkv, mu_wag, w0, a0, k_k, qg, kg, e, cos, sin):
    mi = _mod_index(NS_CTX)
    nblk8 = ROWS // 8
    row512 = lambda i: (i, 0)
    half_out = lambda dt: jax.ShapeDtypeStruct((ROWS, HALF), dt)
    out_shape = [half_out(BF16)] * 4 + [half_out(F32)] * 2 + [half_out(BF16)] * 3 + [
                                 jax.ShapeDtypeStruct((ROWS, HALF), BF16),
                                 jax.ShapeDtypeStruct((ROWS, 256), BF16),
                                 jax.ShapeDtypeStruct((ROWS, 256), BF16),
                                 jax.ShapeDtypeStruct((ROWS, 128), F32),
                                 jax.ShapeDtypeStruct((ROWS, 128), F32)]
    out_specs = [pl.BlockSpec((TS, HALF), row512)] * 10 + [pl.BlockSpec((TS, 256), row512)] * 2 + \
                [pl.BlockSpec((TS, 128), row512)] * 2
    return pl.pallas_call(
        _pre_cd_body,
        out_shape=out_shape,
        grid=(NS,),
        in_specs=[pl.BlockSpec((TS, D), row512),
                  pl.BlockSpec((8, D), lambda i: (jnp.maximum(i * (TS // 8) - 1, 0), 0)),
                  pl.BlockSpec((8, D), lambda i: (jnp.minimum((i + 1) * (TS // 8), nblk8 - 1), 0)),
                  pl.BlockSpec((1, 6, D), lambda i: (mi(i, TS), 0, 0)),
                  _const_spec((1, D)), _const_spec(win.shape),
                  _const_spec(lw1.shape), _const_spec(lw2.shape), _const_spec(la1.shape), _const_spec(la2.shape),
                  _const_spec(lg1.shape), _const_spec(lg2.shape),
                  _const_spec((3, HALF)), _const_spec((3, D)), _const_spec((2, HALF)), _const_spec((2, HALF)),
                  _const_spec((1, HALF)), _const_spec((1, HALF)), _const_spec((1, 256)),
                  _const_spec((HALF, HALF)),
                  pl.BlockSpec((TS, 128), lambda i: (i % (L_LAT // TS), 0)),
                  pl.BlockSpec((TS, 128), lambda i: (i % (L_LAT // TS), 0))],
        out_specs=out_specs,
        scratch_shapes=[pltpu.VMEM((TS, D), F32)],
        compiler_params=_params(("arbitrary",)),
        name="pre_cd",
    )(x, x, x, mod, g, win, lw1, lw2, la1, la2, lg1, lg2, mu_rkv, mu_wag, w0, a0, k_k, qg, kg, e, cos, sin)


def _scan_unit(rc, kc, vc, kkc, lwc, ac, ka, get_state, put_state, delay, reverse, masks):
    cum_mask, tri_s, tri_i, eye, lane_lo, diag8, offs = masks
    lw_hi = lwc.astype(BF16)
    res = lwc - lw_hi.astype(F32)
    lw_mid = res.astype(BF16)
    lw_lo = (res - lw_mid.astype(F32)).astype(BF16)
    cum3 = _dot(cum_mask, jnp.concatenate([lw_hi, lw_mid, lw_lo], axis=1))
    cum = cum3[:, 0:128] + cum3[:, 128:256] + cum3[:, 256:384]
    cume = cum - lwc
    if reverse:
        tot, mid = cum[0:1, :], cum[SC // 2:SC // 2 + 1, :]
    else:
        tot, mid = cum[SC - 1:SC, :], cum[SC // 2 - 1:SC // 2, :]
    at = -kkc * jnp.exp(cume - mid)
    rt = rc * jnp.exp(cum - mid)
    e3 = jnp.exp(mid - cum)
    kt = kc * (1.0 + (ac - 1.0) * ka) * e3
    bt = kkc * ac * e3

    def stack(x):
        return jnp.concatenate([jnp.where(lane_lo, x, 0.0), jnp.where(lane_lo, 0.0, x)], axis=0)

    a2 = stack(at)
    r2 = stack(rt)
    v2 = stack(vc).astype(BF16)
    kb = jnp.concatenate([stack(kt), stack(bt)], axis=0).astype(BF16)
    lhs = jnp.concatenate([a2, r2], axis=0).astype(BF16)
    rhs = jnp.concatenate([bt, bt, kt, kt], axis=0).astype(BF16)
    yield
    p = _dot_nt(lhs, rhs)
    a_ab = jnp.where(tri_s, p[0:128, 0:128], 0.0)
    a_ak = jnp.where(tri_s, p[0:128, 128:256], 0.0)
    a_rb = jnp.where(tri_i, p[128:256, 0:128], 0.0)
    a_rk = jnp.where(tri_i, p[128:256, 128:256], 0.0)

    lk = jnp.where(diag8, a_ab, 0.0)
    tinv = eye + lk
    for _ in range(2):
        lkb = lk.astype(BF16)
        yield
        lk = _dot(lkb, lkb)
        yield
        tinv = _dot(tinv.astype(BF16), (eye + lk).astype(BF16))
    for off in offs:
        tb = tinv.astype(BF16)
        yield
        tl = _dot(tb, jnp.where(off, a_ab, 0.0).astype(BF16)).astype(BF16)
        yield
        tinv = tinv + _dot(tl, tb)

    akv = _dot(a_ak.astype(BF16), v2)
    yield
    wu = _dot(tinv.astype(BF16), jnp.concatenate([a2, akv], axis=1).astype(BF16))
    w2 = wu[:, 0:128].astype(BF16)
    u02 = wu[:, 128:256]
    a_ru = jnp.concatenate([a_rk, a_rb], axis=1).astype(BF16)
    r2b = r2.astype(BF16)

    for _ in range(delay):
        yield
    s_prev = get_state()
    sdb = (s_prev * jnp.exp(mid)).astype(BF16)
    yield
    u2 = u02 + _dot_nt(w2, sdb)
    vu = jnp.concatenate([v2, u2.astype(BF16)], axis=0)
    yield
    y2 = _dot(a_ru, vu) + _dot_nt(r2b, sdb)
    s_new = s_prev * jnp.exp(tot) + _dot_tn(vu, kb) * jnp.exp(tot - mid)
    put_state(s_new)
    return y2[0:SC, :] + y2[SC:2 * SC, :]


def _run_lockstep(gens):
    results = [None] * len(gens)
    live = list(range(len(gens)))
    while live:
        still = []
        for u in live:
            try:
                next(gens[u])
                still.append(u)
            except StopIteration as stop:
                results[u] = stop.value
        live = still
    return results


def _scan_body(tab_ref, *refs):
    n_in = NMEM * 2 * 6
    in_refs = refs[:n_in]
    ka_ref = refs[n_in]
    s0_refs = refs[n_in + 1:n_in + 1 + 2 * NMEM]
    yf_ref, yb_ref, sf_ref, sb_ref, st_ref = refs[n_in + 1 + 2 * NMEM:]
    t = pl.program_id(0)
    first = tab_ref[_T_FIRST, t]
    last = tab_ref[_T_LAST, t]
    is_lat = tab_ref[_T_LAT, t]

    def unit(m, d, p):
        return (m * 2 + d) * NPAIR + p

    @pl.when(first == 1)
    def _():
        zero = jnp.zeros((HD, HD), F32)
        lat_f = is_lat.astype(F32)
        for m in range(NMEM):
            for d in range(2):
                for p in range(NPAIR):
                    lo = s0_refs[m * 2 + d][0, 2 * p] * lat_f
                    hi = s0_refs[m * 2 + d][0, 2 * p + 1] * lat_f
                    st_ref[unit(m, d, p)] = jnp.concatenate(
                        [jnp.concatenate([lo, zero], axis=1), jnp.concatenate([zero, hi], axis=1)], axis=0)

    ri = lax.broadcasted_iota(jnp.int32, (SC, SC), 0)
    ci = lax.broadcasted_iota(jnp.int32, (SC, SC), 1)
    r2 = lax.broadcasted_iota(jnp.int32, (128, 128), 0)
    c2 = lax.broadcasted_iota(jnp.int32, (128, 128), 1)
    same = (r2 & SC) == (c2 & SC)
    eye = (r2 == c2).astype(F32)
    diag8 = (r2 >> 3) == (c2 >> 3)
    lane_lo = lax.broadcasted_iota(jnp.int32, (SC, 128), 1) < HD

    dir_masks = []
    for reverse in (False, True):
        early, late = (r2, c2) if reverse else (c2, r2)
        offs = tuple(((r2 >> sh + 1) == (c2 >> sh + 1)) & ((late & (1 << sh)) != 0) & ((early & (1 << sh)) == 0)
                     for sh in (3, 4, 5))
        if reverse:
            dir_masks.append(((ri <= ci).astype(BF16), same & (r2 < c2), same & (r2 <= c2), eye, lane_lo, diag8, offs))
        else:
            dir_masks.append(((ri >= ci).astype(BF16), same & (r2 > c2), same & (r2 >= c2), eye, lane_lo, diag8, offs))

    state = {}
    gens, dests = [], []
    for sub in range(SUB):
        for d, reverse in enumerate((False, True)):
            y_ref = yb_ref if reverse else yf_ref
            rows = pl.ds((SUB - 1 - sub if reverse else sub) * SC, SC)
            for m in range(NMEM):
                unit_refs = in_refs[(m * 2 + d) * 6:(m * 2 + d + 1) * 6]
                for p in range(NPAIR):
                    u = unit(m, d, p)
                    ls = slice(p * 128, (p + 1) * 128)
                    rc, kc, vc, kkc, lwc, ac = (ref[rows, ls].astype(F32) for ref in unit_refs)
                    get_state = functools.partial(state.get, u) if sub else functools.partial(lambda u: st_ref[u], u)
                    gens.append(_scan_unit(rc, kc, vc, kkc, lwc, ac, ka_ref[:, ls], get_state,
                                           functools.partial(state.__setitem__, u), 2 * sub, reverse, dir_masks[d]))
                    dests.append((y_ref, m, rows, ls))
    for (y_ref, m, rows, ls), y in zip(dests, _run_lockstep(gens)):
        y_ref[m, 0, rows, ls] = y.astype(BF16)
    for u, s_new in state.items():
        st_ref[u] = s_new

    @pl.when(last == 1)
    def _():
        for m in range(NMEM):
            for d, s_out in enumerate((sf_ref, sb_ref)):
                for p in range(NPAIR):
                    s = st_ref[unit(m, d, p)]
                    s_out[m, 0, 2 * p] = s[0:HD, 0:HD]
                    s_out[m, 0, 2 * p + 1] = s[HD:2 * HD, HD:2 * HD]


_T_FWD, _T_BWD = 0, NMEM
_T_YF, _T_YB = 2 * NMEM, 2 * NMEM + 1
_T_FIRST, _T_LAST, _T_LAT = 2 * NMEM + 2, 2 * NMEM + 3, 2 * NMEM + 4
_T_SIN = 2 * NMEM + 5
_T_SOUT = 3 * NMEM + 5
_Y_SLOTS_CTX = ROWS_CTX // SCB // NMEM


def _scan_tables():
    cols = []
    for group, (nb, ln, base, slot0) in enumerate(((N_CTX, L_CTX, 0, 0), (N_LAT, L_LAT, ROWS_CTX, _Y_SLOTS_CTX))):
        nc = ln // SCB
        for p in range(nb // NMEM):
            batches = [p + m * (nb // NMEM) for m in range(NMEM)]
            blks = [(base + b * ln) // SCB for b in batches]
            for c in range(nc):
                col = [blk + c for blk in blks] + [blk + nc - 1 - c for blk in blks]
                col += [slot0 + p * nc + c, slot0 + p * nc + nc - 1 - c]
                col += [int(c == 0), int(c == nc - 1), group]
                col += [b if group == 1 else 0 for b in batches]
                col += [p if group == 0 else N_CTX // NMEM]
                cols.append(col)
    return np.asarray(cols, np.int32).T


def _scan(r, k, v, kk, lwf, lwb, af, ab, k_a, s0f, s0b):
    tab = _scan_tables()
    steps = tab.shape[1]

    def row_spec(table_row):
        return pl.BlockSpec((SCB, HALF), lambda t, tab: (tab[table_row, t], 0))

    def state_spec(table_row):
        return pl.BlockSpec((1, 2 * NPAIR, HD, HD), lambda t, tab: (tab[table_row, t], 0, 0, 0))

    in_specs, args = [], []
    for m in range(NMEM):
        for table_row, lw, a in ((_T_FWD + m, lwf, af), (_T_BWD + m, lwb, ab)):
            in_specs += [row_spec(table_row)] * 6
            args += [r, k, v, kk, lw, a]
    in_specs.append(pl.BlockSpec((1, HALF), lambda t, tab: (0, 0)))
    args.append(k_a)
    for m in range(NMEM):
        in_specs += [state_spec(_T_SIN + m)] * 2
        args += [s0f, s0b]
    y_shape = jax.ShapeDtypeStruct((NMEM, ROWS // SCB // NMEM, SCB, HALF), BF16)
    s_shape = jax.ShapeDtypeStruct((NMEM, N_CTX // NMEM + 1, 2 * NPAIR, HD, HD), F32)
    s_out = pl.BlockSpec((NMEM, 1, 2 * NPAIR, HD, HD), lambda t, tab: (0, tab[_T_SOUT, t], 0, 0, 0))
    grid_spec = pltpu.PrefetchScalarGridSpec(
        num_scalar_prefetch=1,
        grid=(steps,),
        in_specs=in_specs,
        out_specs=[pl.BlockSpec((NMEM, 1, SCB, HALF), lambda t, tab: (0, tab[_T_YF, t], 0, 0)),
                   pl.BlockSpec((NMEM, 1, SCB, HALF), lambda t, tab: (0, tab[_T_YB, t], 0, 0)),
                   s_out, s_out],
        scratch_shapes=[pltpu.VMEM((NMEM * 2 * NPAIR, 128, 128), F32)],
    )
    return pl.pallas_call(
        _scan_body,
        out_shape=[y_shape, y_shape, s_shape, s_shape],
        grid_spec=grid_spec,
        compiler_params=_params(("arbitrary",)),
        name="wkv_scan",
    )(jnp.asarray(tab), *args)


def _attn_body(q_ref, k_ref, v_ref, *rest, cached):
    if cached:
        ck_ref, cv_ref, o_ref = rest
    else:
        (o_ref,) = rest
    lane_lo = lax.broadcasted_iota(jnp.int32, (QB, 128), 1) < HD

    def head(pr, half):
        qp = q_ref[:, pr * 128:(pr + 1) * 128]
        kvh = (2 * pr) // 4
        ks = slice(kvh * 128, (kvh + 1) * 128)
        qh = jnp.where(lane_lo if half == 0 else jnp.logical_not(lane_lo), qp, jnp.zeros_like(qp))
        s_new = _dot_nt(qh, k_ref[:, ks])
        if cached:
            s_old = _dot_nt(qh, ck_ref[0, :, ks])
            yield
            m = jnp.maximum(jnp.max(s_new, -1, keepdims=True), jnp.max(s_old, -1, keepdims=True))
            p_new = jnp.exp(s_new - m)
            p_old = jnp.exp(s_old - m)
            den = jnp.sum(p_new, -1, keepdims=True) + jnp.sum(p_old, -1, keepdims=True)
            yield
            o = _dot(p_new.astype(BF16), v_ref[:, ks]) + _dot(p_old.astype(BF16), cv_ref[0, :, ks])
        else:
            yield
            m = jnp.max(s_new, -1, keepdims=True)
            p_new = jnp.exp(s_new - m)
            den = jnp.sum(p_new, -1, keepdims=True)
            yield
            o = _dot(p_new.astype(BF16), v_ref[:, ks])
        return o / den

    outs = _run_lockstep([head(pr, half) for pr in range(NPAIR) for half in range(2)])
    for pr in range(NPAIR):
        o_ref[:, pr * 128:(pr + 1) * 128] = jnp.where(lane_lo, outs[2 * pr], outs[2 * pr + 1]).astype(BF16)


def _attention(q, kd, vd, base, nb, ln, cache=None):
    nq = ln // QB
    blk0 = base // QB
    seq0 = base // ln
    in_specs = [pl.BlockSpec((QB, HALF), lambda b, j: (blk0 + b * nq + j, 0)),
                pl.BlockSpec((ln, 256), lambda b, j: (seq0 + b, 0)),
                pl.BlockSpec((ln, 256), lambda b, j: (seq0 + b, 0))]
    args = [q, kd, vd]
    if cache is not None:
        in_specs += [pl.BlockSpec((1, PAST, 256), lambda b, j: (b, 0, 0))] * 2
        args += list(cache)
    return pl.pallas_call(
        functools.partial(_attn_body, cached=cache is not None),
        out_shape=jax.ShapeDtypeStruct((nb * ln, HALF), BF16),
        grid=(nb, nq),
        in_specs=in_specs,
        out_specs=pl.BlockSpec((QB, HALF), lambda b, j: (b * nq + j, 0)),
        compiler_params=_params(("arbitrary", "arbitrary")),
        name="gqa_cached" if cache is not None else "gqa",
    )(*args)


def _post_cd_body(x_ref, mod_ref, yf_ref, yb_ref, r_ref, k_ref, v_ref, af_ref, ab_ref, gg_ref,
                  atc_ref, atl_ref, wout_ref, e_ref, lng_ref, lnb_ref, rk_ref, ka_ref, o_ref):
    i = pl.program_id(0)
    gate = mod_ref[0, 2:3, :]
    y = yf_ref[0].reshape(TS, HALF).astype(F32) + yb_ref[0].reshape(TS, HALF).astype(F32)
    yc = y - _head_sum(y, e_ref) * (1.0 / HD)
    yn = yc * lax.rsqrt(_head_sum(yc * yc, e_ref) * (1.0 / HD) + GN_EPS) * lng_ref[...] + lnb_ref[...]
    f32 = lambda ref: ref[...].astype(F32)
    kd2 = f32(k_ref) * (2.0 + (f32(af_ref) + f32(ab_ref) - 2.0) * ka_ref[...])
    bonus = _head_sum(f32(r_ref) * kd2 * rk_ref[...], e_ref) * f32(v_ref)
    yc_out = ((yn + bonus) * f32(gg_ref)).astype(BF16)
    att = jnp.where(i >= NS_CTX, atl_ref[...], atc_ref[...])
    out = _dot(yc_out, wout_ref[0:HALF, :]) + _dot(att, wout_ref[HALF:2 * HALF, :])
    o_ref[...] = x_ref[...] + gate * out


def _post_cd(x, mod, yf, yb, r, k, v, af, ab, gg, att_c, att_l, wout, e, lng, lnb, r_k, k_a):
    mi = _mod_index(NS_CTX)
    row = lambda i: (i, 0)
    half = pl.BlockSpec((TS, HALF), row)
    ctx_per_mem = NS_CTX // NMEM
    lat_per_mem = (NS - NS_CTX) // NMEM

    def y_index(i):
        j = i - NS_CTX
        return (jnp.where(i < NS_CTX, i // ctx_per_mem, j // lat_per_mem),
                jnp.where(i < NS_CTX, i % ctx_per_mem, ctx_per_mem + j % lat_per_mem), 0, 0)

    y_spec = pl.BlockSpec((1, TS // SCB, SCB, HALF), y_index)
    return pl.pallas_call(
        _post_cd_body,
        out_shape=jax.ShapeDtypeStruct((ROWS, D), F32),
        grid=(NS,),
        in_specs=[pl.BlockSpec((TS, D), row),
                  pl.BlockSpec((1, 6, D), lambda i: (mi(i, TS), 0, 0)), y_spec, y_spec] + [half] * 6 +
                 [pl.BlockSpec((TS, HALF), lambda i: (jnp.minimum(i, NS_CTX - 1), 0)),
                  pl.BlockSpec((TS, HALF), lambda i: (jnp.maximum(i - NS_CTX, 0), 0)),
                  _layer_spec((D, D), 1), _const_spec((HALF, HALF)),
                  _const_spec((1, HALF)), _const_spec((1, HALF)), _const_spec((1, HALF)), _const_spec((1, HALF))],
        out_specs=pl.BlockSpec((TS, D), row),
        compiler_params=_params(("arbitrary",)),
        name="post_cd",
    )(x, mod, yf, yb, r, k, v, af, ab, gg, att_c, att_l, wout, e, lng, lnb, r_k, k_a)


def _rope_tables():
    half = 16
    inv = 10000.0 ** (-jnp.arange(half, dtype=F32) / half)
    t = jnp.arange(L_LAT)
    cos_parts, sin_parts = [], []
    for pos in (t // GRID_W, t % GRID_W):
        ang = pos.astype(F32)[:, None] * inv[None, :]
        cos_parts += [jnp.cos(ang), jnp.cos(ang)]
        sin_parts += [-jnp.sin(ang), jnp.sin(ang)]
    cos = jnp.concatenate(cos_parts * 2, axis=1)
    sin = jnp.concatenate(sin_parts * 2, axis=1)
    return cos, sin


def kernel(x_prompt, x_sample, cache_k, cache_v, state_wkv_fwd, state_wkv_bwd, c, c_ctx, mod_w, mod_b, norm1_g, norm2_g, w_out, ffn_up, ffn_conv_w, ffn_conv_b, ffn_down, final_g, ab_w_in, a_vnorm_g, a_vnorm_b, a_ws, a_bs, b_conv_w, b_conv_b, b_norm_g, cd_w_in, c_mu_rkv, c_mu_wag, c_w0, c_w1, c_w2, c_a0, c_a1, c_a2, c_g1, c_g2, c_k_k, c_k_a, c_r_k, c_ln_g, c_ln_b, d_q_g, d_k_g):
    row = lambda a: a.reshape(1, -1)
    cvec =jnp.concatenate([c_ctx[None, :], c, jnp.zeros((16 - 1 - N_LAT, D), F32)], axis=0)
    mods = _modulation(cvec, mod_w, mod_b).reshape(2, 16, 6, D)

    bs = jnp.broadcast_to(a_bs[0][:, :, None], (A_GROUPS, CHUNK, 128))
    cw = jnp.concatenate([b_conv_w[0], jnp.zeros((1, HALF), F32)], axis=0)
    w_out_b, ffn_up_b, ffn_down_b = w_out.astype(BF16), ffn_up.astype(BF16), ffn_down.astype(BF16)
    x = _mix_ab(x_prompt.reshape(ROWS_CTX, D), x_sample.reshape(ROWS_LAT, D), mods[0], row(norm1_g[0]),
                ab_w_in[0].astype(BF16), w_out_b,
                a_ws[0].astype(BF16), bs, row(a_vnorm_g[0]), row(a_vnorm_b[0]),
                cw, row(b_conv_b[0]), row(b_norm_g[0]))
    x = _ffn(x, 0, NT, 0, mods[0], row(norm2_g[0]), ffn_up_b, ffn_conv_w, row(ffn_conv_b[0]),
             ffn_down_b, row(final_g), final=False)

    w = cd_w_in[0]
    wk = w[:, 4 * HALF:4 * HALF + 128]
    wv = w[:, 4 * HALF + 128:4 * HALF + 256]
    win = jnp.concatenate([w[:, :4 * HALF], wk[:, :HD], wk[:, :HD], wk[:, HD:], wk[:, HD:],
                           wv[:, :HD], wv[:, :HD], wv[:, HD:], wv[:, HD:]], axis=1).astype(BF16)
    lw1 = jnp.concatenate([c_w1[0, 0], c_w1[0, 1]], axis=1).astype(BF16)
    la1 = jnp.concatenate([c_a1[0, 0], c_a1[0, 1]], axis=1).astype(BF16)
    ones_blk = jnp.asarray(np.kron(np.eye(HALF // HD), np.ones((HD, HD))), BF16)
    cos, sin = _rope_tables()
    k_a = row(c_k_a[0])
    (r, k, v, kk, lwf, lwb, af, ab, gg, q, kd, vd, ck, cv) = _pre_cd(
        x, mods[1], row(norm1_g[1]), win, lw1, c_w2[0].astype(BF16), la1, c_a2[0].astype(BF16),
        c_g1[0].astype(BF16), c_g2[0].astype(BF16), c_mu_rkv[0], c_mu_wag[0], c_w0[0], c_a0[0],
        row(c_k_k[0]), jnp.tile(d_q_g[0], HALF // HD)[None, :], jnp.tile(d_k_g[0], 4)[None, :],
        ones_blk, cos, sin)

    yf, yb, s_f, s_b = _scan(r, k, v, kk, lwf, lwb, af, ab, k_a, state_wkv_fwd[:, 0], state_wkv_bwd[:, 0])

    def dup_cache(t):
        t = t[:, 0].astype(BF16)
        return jnp.concatenate([t[:, :, 0], t[:, :, 0], t[:, :, 1], t[:, :, 1]], axis=-1)

    att_c = _attention(q, kd, vd, 0, N_CTX, L_CTX)
    att_l = _attention(q, kd, vd, ROWS_CTX, N_LAT, L_LAT, cache=(dup_cache(cache_k), dup_cache(cache_v)))

    x = _post_cd(x, mods[1], yf, yb, r, k, v, af, ab, gg, att_c, att_l, w_out_b, ones_blk,
                 row(c_ln_g[0]), row(c_ln_b[0]), row(c_r_k[0]), k_a)
    ffn1 = (1, mods[1], row(norm2_g[1]), ffn_up_b, ffn_conv_w, row(ffn_conv_b[1]), ffn_down_b, row(final_g))
    y_prompt = _ffn(x, 0, NT_CTX, *ffn1, final=True).reshape(N_CTX, L_CTX, D)
    y_sample = _ffn(x, NT_CTX, NT - NT_CTX, *ffn1, final=True).reshape(N_LAT, L_LAT, D)
    new_k = ck[:ROWS_CTX].reshape(N_CTX, 1, L_CTX, 2, HD)
    new_v = cv[:ROWS_CTX].reshape(N_CTX, 1, L_CTX, 2, HD)
    return (y_prompt, y_sample, new_k, new_v,
            s_f[:, :N_CTX // NMEM].reshape(N_CTX, 1, 2 * NPAIR, HD, HD),
            s_b[:, :N_CTX // NMEM].reshape(N_CTX, 1, 2 * NPAIR, HD, HD))
```

```python
import functools
import math

import jax
import jax.numpy as jnp
import numpy as np
from jax import lax
from jax.experimental import pallas as pl
from jax.experimental.pallas import tpu as pltpu

F32 = jnp.float32
BF16 = jnp.bfloat16

D = 1024
N_CTX, L_CTX = 16, 256
N_LAT, L_LAT = 8, 1024
ROWS_CTX = N_CTX * L_CTX
ROWS_LAT = N_LAT * L_LAT
ROWS = ROWS_CTX + ROWS_LAT
PAST = 256
GRID_W = 64
EPS = 1e-6
GN_EPS = 64e-5
HALF = 512
HD = 64
NPAIR = HALF // 128
D_FF = 2816
B_CONV = 31
CHUNK = 128
A_GROUPS = 4

TM = 1024
NT = ROWS // TM
NT_CTX = ROWS_CTX // TM
SEG = 256
NSEG = TM // SEG
HALO = 16
TS = 512
NS = ROWS // TS
NS_CTX = ROWS_CTX // TS
FB = 256
NJ = D_FF // FB
SC = 64
SUB = 2
SCB = SC * SUB
NMEM = 2
QB = 256
VMEM_LIMIT = 56 * 1024 * 1024

_NT_DIMS = (((1,), (1,)), ((), ()))
_TN_DIMS = (((0,), (0,)), ((), ()))


def _dot(a, b):
    return jnp.dot(a, b, preferred_element_type=F32)


def _dot_nt(a, b):
    return lax.dot_general(a, b, _NT_DIMS, preferred_element_type=F32)


def _dot_tn(a, b):
    return lax.dot_general(a, b, _TN_DIMS, preferred_element_type=F32)


def _rms(x, eps=EPS):
    return x * lax.rsqrt(jnp.mean(x * x, -1, keepdims=True) + eps)


def _sigmoid(x):
    return 1.0 / (1.0 + jnp.exp(-x))


def _silu(x):
    return x * _sigmoid(x)


def _gelu_tanh(x):
    return 0.5 * x * (1.0 + jnp.tanh(math.sqrt(2.0 / math.pi) * (x + 0.044715 * (x * x * x))))


def _const_spec(shape):
    return pl.BlockSpec(shape, lambda *_: (0,) * len(shape), pipeline_mode=pl.Buffered(1))


def _layer_spec(shape, layer):
    return pl.BlockSpec((None,) + tuple(shape), lambda *_: (layer,) + (0,) * len(shape), pipeline_mode=pl.Buffered(1))


def _params(sem):
    return pltpu.CompilerParams(dimension_semantics=sem, vmem_limit_bytes=VMEM_LIMIT)


def _mod_body(c_ref, w_ref, b_ref, o_ref):
    s = _silu(c_ref[...])
    o_ref[0] = jnp.dot(s, w_ref[0], preferred_element_type=F32,
                       precision=lax.Precision.HIGHEST) + b_ref[0]


def _modulation(cvec, mod_w, mod_b):
    depth = mod_w.shape[0]
    nb = 1536
    return pl.pallas_call(
        _mod_body,
        out_shape=jax.ShapeDtypeStruct((depth, 16, 6 * D), F32),
        grid=(depth, 6 * D // nb),
        in_specs=[pl.BlockSpec((16, D), lambda l, j: (0, 0)),
                  pl.BlockSpec((1, D, nb), lambda l, j: (l, 0, j)),
                  pl.BlockSpec((1, 1, nb), lambda l, j: (l, 0, j))],
        out_specs=pl.BlockSpec((1, 16, nb), lambda l, j: (l, 0, j)),
        compiler_params=_params(("arbitrary", "arbitrary")),
        name="modulation",
    )(cvec, mod_w, mod_b.reshape(depth, 1, 6 * D))


def _mod_index(tiles_ctx):
    def idx(i, tile_rows):
        return jnp.where(i >= tiles_ctx, (i - tiles_ctx) // (L_LAT // tile_rows) + 1, 0)
    return idx


def _mix_ab_body(xc_ref, xl_ref, mod_ref, g_ref, win_ref, wout_ref, ws_ref, bs_ref, vg_ref, vb_ref,
                 cw_ref, cb_ref, ng_ref, o_ref, cat_ref, pad_ref, shift_ref):
    i = pl.program_id(0)
    is_lat = i >= NT_CTX
    lat = is_lat.astype(F32)

    def x_rows(r0):
        return jnp.where(is_lat, xl_ref[pl.ds(r0, SEG), :], xc_ref[pl.ds(r0, SEG), :])

    shift = mod_ref[0, 0:1, :]
    scale = mod_ref[0, 1:2, :]
    gate = mod_ref[0, 2:3, :]

    def phase1(s, carry):
        r0 = s * SEG
        x = x_rows(r0)
        h = (_rms(x) * g_ref[...] * (1.0 + scale) + shift).astype(BF16)
        z = _dot(h, win_ref[...])
        za = _gelu_tanh(z[:, :2 * HALF])
        u = za[:, :HALF]
        v = za[:, HALF:]
        vc = v - jnp.mean(v, -1, keepdims=True)
        vn = vc * lax.rsqrt(jnp.mean(vc * vc, -1, keepdims=True) + EPS) * vg_ref[...] + vb_ref[...]
        vnb = vn.astype(BF16)
        for n in range(SEG // CHUNK):
            parts = []
            for g in range(A_GROUPS):
                vv = vnb[n * CHUNK:(n + 1) * CHUNK, g * 128:(g + 1) * 128]
                parts.append(_dot(ws_ref[g], vv) + bs_ref[g])
            vp = jnp.concatenate(parts, axis=1)
            ya = u[n * CHUNK:(n + 1) * CHUNK, :] * vp
            cat_ref[pl.ds(r0 + n * CHUNK, CHUNK), 0:HALF] = ya.astype(BF16)
        zb = z[:, 2 * HALF:]
        pad_ref[s, HALO:HALO + SEG, :] = zb[:, :HALF] * _sigmoid(zb[:, HALF:])
        return carry

    def halos(s):
        zeros_h = jnp.zeros((HALO, HALF), F32)
        left = pad_ref[s - 1, SEG:SEG + HALO, :] * lat if s > 0 else zeros_h
        right = pad_ref[s + 1, HALO:2 * HALO, :] * lat if s < NSEG - 1 else zeros_h
        pad_ref[s, 0:HALO, :] = left
        pad_ref[s, HALO + SEG:2 * HALO + SEG, :] = right

    RB = 32

    shift_rows = SEG + 2 * HALO - 8

    def phase3(s, carry):
        r0 = s * SEG
        for sh in range(1, 8):
            shift_ref[sh - 1] = pad_ref[s, sh:sh + shift_rows, :]
        for rb in range(SEG // RB):
            acc = jnp.zeros((RB, HALF), F32) + cb_ref[...]
            for k in range(B_CONV):
                q8, sh = divmod(HALO - B_CONV // 2 + k, 8)
                off = 8 * q8 + rb * RB
                win = pad_ref[s, off:off + RB, :] if sh == 0 else shift_ref[sh - 1, off:off + RB, :]
                acc = acc + cw_ref[k:k + 1, :] * win
            yb = _silu(_rms(acc) * ng_ref[...])
            cat_ref[pl.ds(r0 + rb * RB, RB), HALF:2 * HALF] = yb.astype(BF16)
        out = _dot(cat_ref[pl.ds(r0, SEG), :], wout_ref[...])
        o_ref[pl.ds(r0, SEG), :] = x_rows(r0) + gate * out
        return carry

    phase1(0, 0)
    for s in range(NSEG):
        if s + 1 < NSEG:
            phase1(s + 1, 0)
        halos(s)
        phase3(s, 0)


def _mix_ab(xc, xl, mod, g, win, wout, ws, bs, vg, vb, cw, cb, ng):
    mi = _mod_index(NT_CTX)
    return pl.pallas_call(
        _mix_ab_body,
        out_shape=jax.ShapeDtypeStruct((ROWS, D), F32),
        grid=(NT,),
        in_specs=[pl.BlockSpec((TM, D), lambda i: (jnp.minimum(i, NT_CTX - 1), 0)),
                  pl.BlockSpec((TM, D), lambda i: (jnp.maximum(i - NT_CTX, 0), 0)),
                  pl.BlockSpec((1, 6, D), lambda i: (mi(i, TM), 0, 0)),
                  _const_spec((1, D)), _const_spec((D, 4 * HALF)), _layer_spec((D, D), 0),
                  _const_spec((A_GROUPS, CHUNK, CHUNK)), _const_spec((A_GROUPS, CHUNK, 128)),
                  _const_spec((1, HALF)), _const_spec((1, HALF)),
                  _const_spec((B_CONV + 1, HALF)), _const_spec((1, HALF)), _const_spec((1, HALF))],
        out_specs=pl.BlockSpec((TM, D), lambda i: (i, 0)),
        scratch_shapes=[pltpu.VMEM((TM, D), BF16),
                        pltpu.VMEM((NSEG, SEG + 2 * HALO, HALF), F32),
                        pltpu.VMEM((7, SEG + 2 * HALO - 8, HALF), F32)],
        compiler_params=_params(("arbitrary",)),
        name="mix_ab",
    )(xc, xl, mod, g, win, wout, ws, bs, vg, vb, cw, cb, ng)


def _ffn_body(x_ref, mod_ref, g_ref, up_ref, cw_ref, cb_ref, down_ref, fg_ref, o_ref,
              h_ref, gated_ref, *, tile0, final):
    is_ctx = tile0 + pl.program_id(0) < NT_CTX
    shift = mod_ref[0, 3:4, :]
    scale = mod_ref[0, 4:5, :]
    h_ref[...] = (_rms(x_ref[...]) * g_ref[...] * (1.0 + scale) + shift).astype(BF16)

    def up_proj(col):
        return _dot(h_ref[...], up_ref[:, col * FB:(col + 1) * FB])

    zero_row = jnp.zeros((1, FB), F32)
    SLAB = 16

    def taps(col):
        cs = slice(col * FB, (col + 1) * FB)
        return cw_ref[0:1, cs], cw_ref[1:2, cs], cw_ref[2:3, cs], cb_ref[:, cs]

    def conv3(zp, zc, zn, w):
        return w[0] * zp + w[1] * zc + w[2] * zn + w[3]

    def conv_slab(z, a, w):
        if a % L_CTX == 0:
            prev = zero_row if a == 0 else jnp.where(is_ctx, zero_row, z[a - 1:a, :])
            zp = jnp.concatenate([prev, z[a:a + SLAB - 1, :]], axis=0)
            zn = z[a + 1:a + SLAB + 1, :]
        else:
            nxt = zero_row if a + SLAB == TM else jnp.where(is_ctx, zero_row, z[a + SLAB:a + SLAB + 1, :])
            zp = z[a - 1:a + SLAB - 1, :]
            zn = jnp.concatenate([z[a + 1:a + SLAB, :], nxt], axis=0)
        return conv3(zp, z[a:a + SLAB, :], zn, w)

    def gate(ca, cb):
        return (_silu(cb) * ca).astype(BF16)

    for j in range(NJ):
        js = slice(j * FB, (j + 1) * FB)
        za, zb = up_proj(j), up_proj(NJ + j)
        wa, wb = taps(j), taps(NJ + j)
        gated_ref[:, js] = gate(conv3(pltpu.roll(za, 1, 0), za, pltpu.roll(za, TM - 1, 0), wa),
                                conv3(pltpu.roll(zb, 1, 0), zb, pltpu.roll(zb, TM - 1, 0), wb))
        for t0 in range(0, TM, L_CTX):
            for a in (t0, t0 + L_CTX - SLAB):
                gated_ref[a:a + SLAB, js] = gate(conv_slab(za, a, wa), conv_slab(zb, a, wb))

    y = x_ref[...] + mod_ref[0, 5:6, :] * _dot(gated_ref[...], down_ref[...])
    if final:
        y = _rms(y) * fg_ref[...]
    o_ref[...] = y


def _ffn(x, tile0, ntiles, layer, mod, g, up, cw, cb, down, fg, final):
    mi = _mod_index(NT_CTX)
    return pl.pallas_call(
        functools.partial(_ffn_body, tile0=tile0, final=final),
        out_shape=jax.ShapeDtypeStruct((ntiles * TM, D), F32),
        grid=(ntiles,),
        in_specs=[pl.BlockSpec((TM, D), lambda i: (tile0 + i, 0)),
                  pl.BlockSpec((1, 6, D), lambda i: (mi(tile0 + i, TM), 0, 0)),
                  _const_spec((1, D)),
                  _layer_spec((D, 2 * D_FF), layer), _layer_spec((3, 2 * D_FF), layer), _const_spec((1, 2 * D_FF)),
                  _layer_spec((D_FF, D), layer), _const_spec((1, D))],
        out_specs=pl.BlockSpec((TM, D), lambda i: (i, 0)),
        scratch_shapes=[pltpu.VMEM((TM, D), BF16), pltpu.VMEM((TM, D_FF), BF16)],
        compiler_params=_params(("arbitrary",)),
        name="conv_ffn",
    )(x, mod, g, up, cw, cb, down, fg)


def _head_sum(x, e_ref):
    return _dot(x.astype(BF16), e_ref[...])


def _swap16(x):
    lane = lax.broadcasted_iota(jnp.int32, x.shape, 1)
    n = x.shape[1]
    return jnp.where((lane & 16) == 0, pltpu.roll(x, n - 16, 1), pltpu.roll(x, 16, 1))


def _pre_cd_body(x_ref, xp_ref, xn_ref, mod_ref, g_ref, win_ref, lw1_ref, lw2_ref, la1_ref, la2_ref,
                 lg1_ref, lg2_ref, mu_rkv_ref, mu_wag_ref, w0_ref, a0_ref, kk_ref, qg_ref, kg_ref,
                 e_ref, cos_ref, sin_ref,
                 r_ref, k_ref, v_ref, kkn_ref, lwf_ref, lwb_ref, af_ref, ab_ref, gg_ref,
                 q_ref, kd_ref, vd_ref, ck_ref, cv_ref, sh_ref):
    i = pl.program_id(0)
    is_lat = i >= NS_CTX
    shift = mod_ref[0, 0:1, :]
    scale = mod_ref[0, 1:2, :]
    gn = g_ref[...]

    def adaln(x):
        return _rms(x) * gn * (1.0 + scale) + shift

    h = adaln(x_ref[...])
    hh = adaln(jnp.concatenate([xp_ref[...], xn_ref[...]], axis=0))
    hb = h.astype(BF16)

    seq = jnp.where(is_lat, L_LAT, L_CTX)
    has_prev = (((i * TS) & (seq - 1)) != 0).astype(F32)
    has_next = ((((i + 1) * TS) & (seq - 1)) != 0).astype(F32)

    def token_shift(z, zh, mix):
        n = z.shape[1]
        dst = sh_ref.at[:, 0:n]
        dst[...] = mix(z, 0.5 * (pltpu.roll(z, 1, 0) + pltpu.roll(z, TS - 1, 0)) - z)

        def fixed(t, nb):
            zt = z[t:t + 1, :]
            return mix(zt, 0.5 * nb - zt)

        dst[0:1, :] = fixed(0, zh[7:8, :] * has_prev + z[1:2, :])
        dst[TS - 1:TS, :] = fixed(TS - 1, z[TS - 2:TS - 1, :] + zh[8:9, :] * has_next)
        for t in range(L_CTX, TS, L_CTX):
            dst[t - 1:t, :] = jnp.where(is_lat, dst[t - 1:t, :], fixed(t - 1, z[t - 2:t - 1, :]))
            dst[t:t + 1, :] = jnp.where(is_lat, dst[t:t + 1, :], fixed(t, z[t + 1:t + 2, :]))
        return dst[...]

    z = _dot(hb, win_ref[:, 0:3 * HALF])
    zq = _dot(hb, win_ref[:, 3 * HALF:4 * HALF])
    zk = _dot(hb, win_ref[:, 4 * HALF:4 * HALF + 256])
    zv = _dot(hb, win_ref[:, 4 * HALF + 256:4 * HALF + 512])
    zh = _dot(hh.astype(BF16), win_ref[:, 0:3 * HALF])

    dh = token_shift(h, hh, lambda zc, dz: dz)
    xw = (h + dh * mu_wag_ref[0:1, :]).astype(BF16)
    xa = (h + dh * mu_wag_ref[1:2, :]).astype(BF16)
    xg = (h + dh * mu_wag_ref[2:3, :]).astype(BF16)
    pw = _dot(xw, lw1_ref[...])
    pa = _dot(xa, la1_ref[...])
    pg = _dot(xg, lg1_ref[...])
    q_ss = _head_sum(zq * zq, e_ref)
    k_ss = _head_sum(zk * zk, e_ref[0:256, 0:256])

    outs = []
    for c in range(3):
        mu = mu_rkv_ref[c:c + 1, :]
        outs.append(token_shift(z[:, c * HALF:(c + 1) * HALF], zh[:, c * HALF:(c + 1) * HALF],
                                lambda zc, dz: zc + dz * mu))
    r, k, v = outs
    r_ref[...] = r.astype(BF16)
    k_ref[...] = k.astype(BF16)
    v_ref[...] = v.astype(BF16)
    kk = k * kk_ref[...]
    kkn_ref[...] = (kk * lax.rsqrt(_head_sum(kk * kk, e_ref) + 1e-12)).astype(BF16)

    tw = jnp.tanh(pw).astype(BF16)
    ta = pa.astype(BF16)
    for d, (lw_out, a_out) in enumerate(((lwf_ref, af_ref), (lwb_ref, ab_ref))):
        wl = w0_ref[d:d + 1, :] + _dot(tw[:, d * 64:(d + 1) * 64], lw2_ref[d])
        lw_out[...] = (-math.exp(-0.5)) * _sigmoid(wl)
        a_out[...] = _sigmoid(a0_ref[d:d + 1, :] + _dot(ta[:, d * 64:(d + 1) * 64], la2_ref[d])).astype(BF16)
    gg_ref[...] = _dot(_sigmoid(pg).astype(BF16), lg2_ref[...]).astype(BF16)

    lat_f = is_lat.astype(F32)
    cos = cos_ref[...] * lat_f + (1.0 - lat_f)
    sin = sin_ref[...] * lat_f
    qn = zq * lax.rsqrt(q_ss * (1.0 / HD) + EPS) * qg_ref[...]
    cos4 = jnp.concatenate([cos] * NPAIR, axis=1)
    sin4 = jnp.concatenate([sin] * NPAIR, axis=1)
    q_ref[...] = ((qn * cos4 + _swap16(qn) * sin4) * (HD ** -0.5)).astype(BF16)

    kn = zk * lax.rsqrt(k_ss * (1.0 / HD) + EPS) * kg_ref[...]
    lane = lax.broadcasted_iota(jnp.int32, (TS, 128), 1)
    ck_ref[...] = jnp.where(lane < HD, kn[:, 0:128], kn[:, 128:256])
    cos2 = jnp.concatenate([cos, cos], axis=1)
    sin2 = jnp.concatenate([sin, sin], axis=1)
    kd_ref[...] = (kn * cos2 + _swap16(kn) * sin2).astype(BF16)
    cv_ref[...] = jnp.where(lane < HD, zv[:, 0:128], zv[:, 128:256])
    vd_ref[...] = zv.astype(BF16)


def _pre_cd(x, mod, g, win, lw1, lw2, la1, la2, lg1, lg2, mu_rkv, mu_wag, w0, a0, k_k, qg, kg, e, cos, sin):
    mi = _mod_index(NS_CTX)
    nblk8 = ROWS // 8
    row512 = lambda i: (i, 0)
    half_out = lambda dt: jax.ShapeDtypeStruct((ROWS, HALF), dt)
    out_shape = [half_out(BF16)] * 4 + [half_out(F32)] * 2 + [half_out(BF16)] * 3 + [
                                 jax.ShapeDtypeStruct((ROWS, HALF), BF16),
                                 jax.ShapeDtypeStruct((ROWS, 256), BF16),
                                 jax.ShapeDtypeStruct((ROWS, 256), BF16),
                                 jax.ShapeDtypeStruct((ROWS, 128), F32),
                                 jax.ShapeDtypeStruct((ROWS, 128), F32)]
    out_specs = [pl.BlockSpec((TS, HALF), row512)] * 10 + [pl.BlockSpec((TS, 256), row512)] * 2 + \
                [pl.BlockSpec((TS, 128), row512)] * 2
    return pl.pallas_call(
        _pre_cd_body,
        out_shape=out_shape,
        grid=(NS,),
        in_specs=[pl.BlockSpec((TS, D), row512),
                  pl.BlockSpec((8, D), lambda i: (jnp.maximum(i * (TS // 8) - 1, 0), 0)),
                  pl.BlockSpec((8, D), lambda i: (jnp.minimum((i + 1) * (TS // 8), nblk8 - 1), 0)),
                  pl.BlockSpec((1, 6, D), lambda i: (mi(i, TS), 0, 0)),
                  _const_spec((1, D)), _const_spec(win.shape),
                  _const_spec(lw1.shape), _const_spec(lw2.shape), _const_spec(la1.shape), _const_spec(la2.shape),
                  _const_spec(lg1.shape), _const_spec(lg2.shape),
                  _const_spec((3, HALF)), _const_spec((3, D)), _const_spec((2, HALF)), _const_spec((2, HALF)),
                  _const_spec((1, HALF)), _const_spec((1, HALF)), _const_spec((1, 256)),
                  _const_spec((HALF, HALF)),
                  pl.BlockSpec((TS, 128), lambda i: (i % (L_LAT // TS), 0)),
                  pl.BlockSpec((TS, 128), lambda i: (i % (L_LAT // TS), 0))],
        out_specs=out_specs,
        scratch_shapes=[pltpu.VMEM((TS, D), F32)],
        compiler_params=_params(("arbitrary",)),
        name="pre_cd",
    )(x, x, x, mod, g, win, lw1, lw2, la1, la2, lg1, lg2, mu_rkv, mu_wag, w0, a0, k_k, qg, kg, e, cos, sin)


def _scan_unit(rc, kc, vc, kkc, lwc, ac, ka, get_state, put_state, delay, reverse, masks):
    cum_mask, tri_s, tri_i, eye, lane_lo, diag8, offs = masks
    lw_hi = lwc.astype(BF16)
    res = lwc - lw_hi.astype(F32)
    lw_mid = res.astype(BF16)
    lw_lo = (res - lw_mid.astype(F32)).astype(BF16)
    cum3 = _dot(cum_mask, jnp.concatenate([lw_hi, lw_mid, lw_lo], axis=1))
    cum = cum3[:, 0:128] + cum3[:, 128:256] + cum3[:, 256:384]
    cume = cum - lwc
    if reverse:
        tot, mid = cum[0:1, :], cum[SC // 2:SC // 2 + 1, :]
    else:
        tot, mid = cum[SC - 1:SC, :], cum[SC // 2 - 1:SC // 2, :]
    at = -kkc * jnp.exp(cume - mid)
    rt = rc * jnp.exp(cum - mid)
    e3 = jnp.exp(mid - cum)
    kt = kc * (1.0 + (ac - 1.0) * ka) * e3
    bt = kkc * ac * e3

    def stack(x):
        return jnp.concatenate([jnp.where(lane_lo, x, 0.0), jnp.where(lane_lo, 0.0, x)], axis=0)

    a2 = stack(at)
    r2 = stack(rt)
    v2 = stack(vc).astype(BF16)
    kb = jnp.concatenate([stack(kt), stack(bt)], axis=0).astype(BF16)
    lhs = jnp.concatenate([a2, r2], axis=0).astype(BF16)
    rhs = jnp.concatenate([bt, bt, kt, kt], axis=0).astype(BF16)
    yield
    p = _dot_nt(lhs, rhs)
    a_ab = jnp.where(tri_s, p[0:128, 0:128], 0.0)
    a_ak = jnp.where(tri_s, p[0:128, 128:256], 0.0)
    a_rb = jnp.where(tri_i, p[128:256, 0:128], 0.0)
    a_rk = jnp.where(tri_i, p[128:256, 128:256], 0.0)

    lk = jnp.where(diag8, a_ab, 0.0)
    tinv = eye + lk
    for _ in range(2):
        lkb = lk.astype(BF16)
        yield
        lk = _dot(lkb, lkb)
        yield
        tinv = _dot(tinv.astype(BF16), (eye + lk).astype(BF16))
    for off in offs:
        tb = tinv.astype(BF16)
        yield
        tl = _dot(tb, jnp.where(off, a_ab, 0.0).astype(BF16)).astype(BF16)
        yield
        tinv = tinv + _dot(tl, tb)

    akv = _dot(a_ak.astype(BF16), v2)
    yield
    wu = _dot(tinv.astype(BF16), jnp.concatenate([a2, akv], axis=1).astype(BF16))
    w2 = wu[:, 0:128].astype(BF16)
    u02 = wu[:, 128:256]
    a_ru = jnp.concatenate([a_rk, a_rb], axis=1).astype(BF16)
    r2b = r2.astype(BF16)

    for _ in range(delay):
        yield
    s_prev = get_state()
    sdb = (s_prev * jnp.exp(mid)).astype(BF16)
    yield
    u2 = u02 + _dot_nt(w2, sdb)
    vu = jnp.concatenate([v2, u2.astype(BF16)], axis=0)
    yield
    y2 = _dot(a_ru, vu) + _dot_nt(r2b, sdb)
    s_new = s_prev * jnp.exp(tot) + _dot_tn(vu, kb) * jnp.exp(tot - mid)
    put_state(s_new)
    return y2[0:SC, :] + y2[SC:2 * SC, :]


def _run_lockstep(gens):
    results = [None] * len(gens)
    live = list(range(len(gens)))
    while live:
        still = []
        for u in live:
            try:
                next(gens[u])
                still.append(u)
            except StopIteration as stop:
                results[u] = stop.value
        live = still
    return results


def _scan_body(tab_ref, *refs):
    n_in = NMEM * 2 * 6
    in_refs = refs[:n_in]
    ka_ref = refs[n_in]
    s0_refs = refs[n_in + 1:n_in + 1 + 2 * NMEM]
    yf_ref, yb_ref, sf_ref, sb_ref, st_ref = refs[n_in + 1 + 2 * NMEM:]
    t = pl.program_id(0)
    first = tab_ref[_T_FIRST, t]
    last = tab_ref[_T_LAST, t]
    is_lat = tab_ref[_T_LAT, t]

    def unit(m, d, p):
        return (m * 2 + d) * NPAIR + p

    @pl.when(first == 1)
    def _():
        zero = jnp.zeros((HD, HD), F32)
        lat_f = is_lat.astype(F32)
        for m in range(NMEM):
            for d in range(2):
                for p in range(NPAIR):
                    lo = s0_refs[m * 2 + d][0, 2 * p] * lat_f
                    hi = s0_refs[m * 2 + d][0, 2 * p + 1] * lat_f
                    st_ref[unit(m, d, p)] = jnp.concatenate(
                        [jnp.concatenate([lo, zero], axis=1), jnp.concatenate([zero, hi], axis=1)], axis=0)

    ri = lax.broadcasted_iota(jnp.int32, (SC, SC), 0)
    ci = lax.broadcasted_iota(jnp.int32, (SC, SC), 1)
    r2 = lax.broadcasted_iota(jnp.int32, (128, 128), 0)
    c2 = lax.broadcasted_iota(jnp.int32, (128, 128), 1)
    same = (r2 & SC) == (c2 & SC)
    eye = (r2 == c2).astype(F32)
    diag8 = (r2 >> 3) == (c2 >> 3)
    lane_lo = lax.broadcasted_iota(jnp.int32, (SC, 128), 1) < HD

    dir_masks = []
    for reverse in (False, True):
        early, late = (r2, c2) if reverse else (c2, r2)
        offs = tuple(((r2 >> sh + 1) == (c2 >> sh + 1)) & ((late & (1 << sh)) != 0) & ((early & (1 << sh)) == 0)
                     for sh in (3, 4, 5))
        if reverse:
            dir_masks.append(((ri <= ci).astype(BF16), same & (r2 < c2), same & (r2 <= c2), eye, lane_lo, diag8, offs))
        else:
            dir_masks.append(((ri >= ci).astype(BF16), same & (r2 > c2), same & (r2 >= c2), eye, lane_lo, diag8, offs))

    state = {}
    gens, dests = [], []
    for sub in range(SUB):
        for d, reverse in enumerate((False, True)):
            y_ref = yb_ref if reverse else yf_ref
            rows = pl.ds((SUB - 1 - sub if reverse else sub) * SC, SC)
            for m in range(NMEM):
                unit_refs = in_refs[(m * 2 + d) * 6:(m * 2 + d + 1) * 6]
                for p in range(NPAIR):
                    u = unit(m, d, p)
                    ls = slice(p * 128, (p + 1) * 128)
                    rc, kc, vc, kkc, lwc, ac = (ref[rows, ls].astype(F32) for ref in unit_refs)
                    get_state = functools.partial(state.get, u) if sub else functools.partial(lambda u: st_ref[u], u)
                    gens.append(_scan_unit(rc, kc, vc, kkc, lwc, ac, ka_ref[:, ls], get_state,
                                           functools.partial(state.__setitem__, u), 2 * sub, reverse, dir_masks[d]))
                    dests.append((y_ref, m, rows, ls))
    for (y_ref, m, rows, ls), y in zip(dests, _run_lockstep(gens)):
        y_ref[m, 0, rows, ls] = y.astype(BF16)
    for u, s_new in state.items():
        st_ref[u] = s_new

    @pl.when(last == 1)
    def _():
        for m in range(NMEM):
            for d, s_out in enumerate((sf_ref, sb_ref)):
                for p in range(NPAIR):
                    s = st_ref[unit(m, d, p)]
                    s_out[m, 0, 2 * p] = s[0:HD, 0:HD]
                    s_out[m, 0, 2 * p + 1] = s[HD:2 * HD, HD:2 * HD]


_T_FWD, _T_BWD = 0, NMEM
_T_YF, _T_YB = 2 * NMEM, 2 * NMEM + 1
_T_FIRST, _T_LAST, _T_LAT = 2 * NMEM + 2, 2 * NMEM + 3, 2 * NMEM + 4
_T_SIN = 2 * NMEM + 5
_T_SOUT = 3 * NMEM + 5
_Y_SLOTS_CTX = ROWS_CTX // SCB // NMEM


def _scan_tables():
    cols = []
    for group, (nb, ln, base, slot0) in enumerate(((N_CTX, L_CTX, 0, 0), (N_LAT, L_LAT, ROWS_CTX, _Y_SLOTS_CTX))):
        nc = ln // SCB
        for p in range(nb // NMEM):
            batches = [p + m * (nb // NMEM) for m in range(NMEM)]
            blks = [(base + b * ln) // SCB for b in batches]
            for c in range(nc):
                col = [blk + c for blk in blks] + [blk + nc - 1 - c for blk in blks]
                col += [slot0 + p * nc + c, slot0 + p * nc + nc - 1 - c]
                col += [int(c == 0), int(c == nc - 1), group]
                col += [b if group == 1 else 0 for b in batches]
                col += [p if group == 0 else N_CTX // NMEM]
                cols.append(col)
    return np.asarray(cols, np.int32).T


def _scan(r, k, v, kk, lwf, lwb, af, ab, k_a, s0f, s0b):
    tab = _scan_tables()
    steps = tab.shape[1]

    def row_spec(table_row):
        return pl.BlockSpec((SCB, HALF), lambda t, tab: (tab[table_row, t], 0))

    def state_spec(table_row):
        return pl.BlockSpec((1, 2 * NPAIR, HD, HD), lambda t, tab: (tab[table_row, t], 0, 0, 0))

    in_specs, args = [], []
    for m in range(NMEM):
        for table_row, lw, a in ((_T_FWD + m, lwf, af), (_T_BWD + m, lwb, ab)):
            in_specs += [row_spec(table_row)] * 6
            args += [r, k, v, kk, lw, a]
    in_specs.append(pl.BlockSpec((1, HALF), lambda t, tab: (0, 0)))
    args.append(k_a)
    for m in range(NMEM):
        in_specs += [state_spec(_T_SIN + m)] * 2
        args += [s0f, s0b]
    y_shape = jax.ShapeDtypeStruct((NMEM, ROWS // SCB // NMEM, SCB, HALF), BF16)
    s_shape = jax.ShapeDtypeStruct((NMEM, N_CTX // NMEM + 1, 2 * NPAIR, HD, HD), F32)
    s_out = pl.BlockSpec((NMEM, 1, 2 * NPAIR, HD, HD), lambda t, tab: (0, tab[_T_SOUT, t], 0, 0, 0))
    grid_spec = pltpu.PrefetchScalarGridSpec(
        num_scalar_prefetch=1,
        grid=(steps,),
        in_specs=in_specs,
        out_specs=[pl.BlockSpec((NMEM, 1, SCB, HALF), lambda t, tab: (0, tab[_T_YF, t], 0, 0)),
                   pl.BlockSpec((NMEM, 1, SCB, HALF), lambda t, tab: (0, tab[_T_YB, t], 0, 0)),
                   s_out, s_out],
        scratch_shapes=[pltpu.VMEM((NMEM * 2 * NPAIR, 128, 128), F32)],
    )
    return pl.pallas_call(
        _scan_body,
        out_shape=[y_shape, y_shape, s_shape, s_shape],
        grid_spec=grid_spec,
        compiler_params=_params(("arbitrary",)),
        name="wkv_scan",
    )(jnp.asarray(tab), *args)


def _attn_body(q_ref, k_ref, v_ref, *rest, cached):
    if cached:
        ck_ref, cv_ref, o_ref = rest
    else:
        (o_ref,) = rest
    lane_lo = lax.broadcasted_iota(jnp.int32, (QB, 128), 1) < HD

    def head(pr, half):
        qp = q_ref[:, pr * 128:(pr + 1) * 128]
        kvh = (2 * pr) // 4
        ks = slice(kvh * 128, (kvh + 1) * 128)
        qh = jnp.where(lane_lo if half == 0 else jnp.logical_not(lane_lo), qp, jnp.zeros_like(qp))
        s_new = _dot_nt(qh, k_ref[:, ks])
        if cached:
            s_old = _dot_nt(qh, ck_ref[0, :, ks])
            yield
            m = jnp.maximum(jnp.max(s_new, -1, keepdims=True), jnp.max(s_old, -1, keepdims=True))
            p_new = jnp.exp(s_new - m)
            p_old = jnp.exp(s_old - m)
            den = jnp.sum(p_new, -1, keepdims=True) + jnp.sum(p_old, -1, keepdims=True)
            yield
            o = _dot(p_new.astype(BF16), v_ref[:, ks]) + _dot(p_old.astype(BF16), cv_ref[0, :, ks])
        else:
            yield
            m = jnp.max(s_new, -1, keepdims=True)
            p_new = jnp.exp(s_new - m)
            den = jnp.sum(p_new, -1, keepdims=True)
            yield
            o = _dot(p_new.astype(BF16), v_ref[:, ks])
        return o / den

    outs = _run_lockstep([head(pr, half) for pr in range(NPAIR) for half in range(2)])
    for pr in range(NPAIR):
        o_ref[:, pr * 128:(pr + 1) * 128] = jnp.where(lane_lo, outs[2 * pr], outs[2 * pr + 1]).astype(BF16)


def _attention(q, kd, vd, base, nb, ln, cache=None):
    nq = ln // QB
    blk0 = base // QB
    seq0 = base // ln
    in_specs = [pl.BlockSpec((QB, HALF), lambda b, j: (blk0 + b * nq + j, 0)),
                pl.BlockSpec((ln, 256), lambda b, j: (seq0 + b, 0)),
                pl.BlockSpec((ln, 256), lambda b, j: (seq0 + b, 0))]
    args = [q, kd, vd]
    if cache is not None:
        in_specs += [pl.BlockSpec((1, PAST, 256), lambda b, j: (b, 0, 0))] * 2
        args += list(cache)
    return pl.pallas_call(
        functools.partial(_attn_body, cached=cache is not None),
        out_shape=jax.ShapeDtypeStruct((nb * ln, HALF), BF16),
        grid=(nb, nq),
        in_specs=in_specs,
        out_specs=pl.BlockSpec((QB, HALF), lambda b, j: (b * nq + j, 0)),
        compiler_params=_params(("arbitrary", "arbitrary")),
        name="gqa_cached" if cache is not None else "gqa",
    )(*args)


def _post_cd_body(x_ref, mod_ref, yf_ref, yb_ref, r_ref, k_ref, v_ref, af_ref, ab_ref, gg_ref,
                  atc_ref, atl_ref, wout_ref, e_ref, lng_ref, lnb_ref, rk_ref, ka_ref, o_ref):
    i = pl.program_id(0)
    gate = mod_ref[0, 2:3, :]
    y = yf_ref[0].reshape(TS, HALF).astype(F32) + yb_ref[0].reshape(TS, HALF).astype(F32)
    yc = y - _head_sum(y, e_ref) * (1.0 / HD)
    yn = yc * lax.rsqrt(_head_sum(yc * yc, e_ref) * (1.0 / HD) + GN_EPS) * lng_ref[...] + lnb_ref[...]
    f32 = lambda ref: ref[...].astype(F32)
    kd2 = f32(k_ref) * (2.0 + (f32(af_ref) + f32(ab_ref) - 2.0) * ka_ref[...])
    bonus = _head_sum(f32(r_ref) * kd2 * rk_ref[...], e_ref) * f32(v_ref)
    yc_out = ((yn + bonus) * f32(gg_ref)).astype(BF16)
    att = jnp.where(i >= NS_CTX, atl_ref[...], atc_ref[...])
    out = _dot(yc_out, wout_ref[0:HALF, :]) + _dot(att, wout_ref[HALF:2 * HALF, :])
    o_ref[...] = x_ref[...] + gate * out


def _post_cd(x, mod, yf, yb, r, k, v, af, ab, gg, att_c, att_l, wout, e, lng, lnb, r_k, k_a):
    mi = _mod_index(NS_CTX)
    row = lambda i: (i, 0)
    half = pl.BlockSpec((TS, HALF), row)
    ctx_per_mem = NS_CTX // NMEM
    lat_per_mem = (NS - NS_CTX) // NMEM

    def y_index(i):
        j = i - NS_CTX
        return (jnp.where(i < NS_CTX, i // ctx_per_mem, j // lat_per_mem),
                jnp.where(i < NS_CTX, i % ctx_per_mem, ctx_per_mem + j % lat_per_mem), 0, 0)

    y_spec = pl.BlockSpec((1, TS // SCB, SCB, HALF), y_index)
    return pl.pallas_call(
        _post_cd_body,
        out_shape=jax.ShapeDtypeStruct((ROWS, D), F32),
        grid=(NS,),
        in_specs=[pl.BlockSpec((TS, D), row),
                  pl.BlockSpec((1, 6, D), lambda i: (mi(i, TS), 0, 0)), y_spec, y_spec] + [half] * 6 +
                 [pl.BlockSpec((TS, HALF), lambda i: (jnp.minimum(i, NS_CTX - 1), 0)),
                  pl.BlockSpec((TS, HALF), lambda i: (jnp.maximum(i - NS_CTX, 0), 0)),
                  _layer_spec((D, D), 1), _const_spec((HALF, HALF)),
                  _const_spec((1, HALF)), _const_spec((1, HALF)), _const_spec((1, HALF)), _const_spec((1, HALF))],
        out_specs=pl.BlockSpec((TS, D), row),
        compiler_params=_params(("arbitrary",)),
        name="post_cd",
    )(x, mod, yf, yb, r, k, v, af, ab, gg, att_c, att_l, wout, e, lng, lnb, r_k, k_a)


def _rope_tables():
    half = 16
    inv = 10000.0 ** (-jnp.arange(half, dtype=F32) / half)
    t = jnp.arange(L_LAT)
    cos_parts, sin_parts = [], []
    for pos in (t // GRID_W, t % GRID_W):
        ang = pos.astype(F32)[:, None] * inv[None, :]
        cos_parts += [jnp.cos(ang), jnp.cos(ang)]
        sin_parts += [-jnp.sin(ang), jnp.sin(ang)]
    cos = jnp.concatenate(cos_parts * 2, axis=1)
    sin = jnp.concatenate(sin_parts * 2, axis=1)
    return cos, sin


def kernel(x_prompt, x_sample, cache_k, cache_v, state_wkv_fwd, state_wkv_bwd, c, c_ctx, mod_w, mod_b, norm1_g, norm2_g, w_out, ffn_up, ffn_conv_w, ffn_conv_b, ffn_down, final_g, ab_w_in, a_vnorm_g, a_vnorm_b, a_ws, a_bs, b_conv_w, b_conv_b, b_norm_g, cd_w_in, c_mu_rkv, c_mu_wag, c_w0, c_w1, c_w2, c_a0, c_a1, c_a2, c_g1, c_g2, c_k_k, c_k_a, c_r_k, c_ln_g, c_ln_b, d_q_g, d_k_g):
    row = lambda a: a.reshape(1, -1)
    cvec =jnp.concatenate([c_ctx[None, :], c, jnp.zeros((16 - 1 - N_LAT, D), F32)], axis=0)
    mods = _modulation(cvec, mod_w, mod_b).reshape(2, 16, 6, D)

    bs = jnp.broadcast_to(a_bs[0][:, :, None], (A_GROUPS, CHUNK, 128))
    cw = jnp.concatenate([b_conv_w[0], jnp.zeros((1, HALF), F32)], axis=0)
    w_out_b, ffn_up_b, ffn_down_b = w_out.astype(BF16), ffn_up.astype(BF16), ffn_down.astype(BF16)
    x = _mix_ab(x_prompt.reshape(ROWS_CTX, D), x_sample.reshape(ROWS_LAT, D), mods[0], row(norm1_g[0]),
                ab_w_in[0].astype(BF16), w_out_b,
                a_ws[0].astype(BF16), bs, row(a_vnorm_g[0]), row(a_vnorm_b[0]),
                cw, row(b_conv_b[0]), row(b_norm_g[0]))
    x = _ffn(x, 0, NT, 0, mods[0], row(norm2_g[0]), ffn_up_b, ffn_conv_w, row(ffn_conv_b[0]),
             ffn_down_b, row(final_g), final=False)

    w = cd_w_in[0]
    wk = w[:, 4 * HALF:4 * HALF + 128]
    wv = w[:, 4 * HALF + 128:4 * HALF + 256]
    win = jnp.concatenate([w[:, :4 * HALF], wk[:, :HD], wk[:, :HD], wk[:, HD:], wk[:, HD:],
                           wv[:, :HD], wv[:, :HD], wv[:, HD:], wv[:, HD:]], axis=1).astype(BF16)
    lw1 = jnp.concatenate([c_w1[0, 0], c_w1[0, 1]], axis=1).astype(BF16)
    la1 = jnp.concatenate([c_a1[0, 0], c_a1[0, 1]], axis=1).astype(BF16)
    ones_blk = jnp.asarray(np.kron(np.eye(HALF // HD), np.ones((HD, HD))), BF16)
    cos, sin = _rope_tables()
    k_a = row(c_k_a[0])
    (r, k, v, kk, lwf, lwb, af, ab, gg, q, kd, vd, ck, cv) = _pre_cd(
        x, mods[1], row(norm1_g[1]), win, lw1, c_w2[0].astype(BF16), la1, c_a2[0].astype(BF16),
        c_g1[0].astype(BF16), c_g2[0].astype(BF16), c_mu_rkv[0], c_mu_wag[0], c_w0[0], c_a0[0],
        row(c_k_k[0]), jnp.tile(d_q_g[0], HALF // HD)[None, :], jnp.tile(d_k_g[0], 4)[None, :],
        ones_blk, cos, sin)

    yf, yb, s_f, s_b = _scan(r, k, v, kk, lwf, lwb, af, ab, k_a, state_wkv_fwd[:, 0], state_wkv_bwd[:, 0])

    def dup_cache(t):
        t = t[:, 0].astype(BF16)
        return jnp.concatenate([t[:, :, 0], t[:, :, 0], t[:, :, 1], t[:, :, 1]], axis=-1)

    att_c = _attention(q, kd, vd, 0, N_CTX, L_CTX)
    att_l = _attention(q, kd, vd, ROWS_CTX, N_LAT, L_LAT, cache=(dup_cache(cache_k), dup_cache(cache_v)))

    x = _post_cd(x, mods[1], yf, yb, r, k, v, af, ab, gg, att_c, att_l, w_out_b, ones_blk,
                 row(c_ln_g[0]), row(c_ln_b[0]), row(c_r_k[0]), k_a)
    ffn1 = (1, mods[1], row(norm2_g[1]), ffn_up_b, ffn_conv_w, row(ffn_conv_b[1]), ffn_down_b, row(final_g))
    y_prompt = _ffn(x, 0, NT_CTX, *ffn1, final=True).reshape(N_CTX, L_CTX, D)
    y_sample = _ffn(x, NT_CTX, NT - NT_CTX, *ffn1, final=True).reshape(N_LAT, L_LAT, D)
    new_k = ck[:ROWS_CTX].reshape(N_CTX, 1, L_CTX, 2, HD)
    new_v = cv[:ROWS_CTX].reshape(N_CTX, 1, L_CTX, 2, HD)
    return (y_prompt, y_sample, new_k, new_v,
            s_f[:, :N_CTX // NMEM].reshape(N_CTX, 1, 2 * NPAIR, HD, HD),
            s_b[:, :N_CTX // NMEM].reshape(N_CTX, 1, 2 * NPAIR, HD, HD))
```
